```python
import math
import jax, jax.numpy as jnp
from jax import lax
import numpy as np

D_MODEL = 1024
BATCH = 8
SEQ = 2048
DEPTH = 4
DEC_BATCH = 128
DEC_SEQ = 1
PAST_LEN = 16384
PAGE_SIZE = 128

N_A_LAYERS = (DEPTH + 1) // 2
N_C_LAYERS = DEPTH // 2
RET_HEADS = 4
RET_DK = 128
RET_DV = 128
RET_WIDTH = RET_HEADS * RET_DV
RET_CHUNK = 128
ROPE_BASE = 10000.0
LRU_WIDTH = D_MODEL // 2
LRU_BLOCKS = 4
LRU_BLOCK = LRU_WIDTH // LRU_BLOCKS
LRU_C = 8.0
CONV_W = 4
MIX_WIDTH = RET_WIDTH + LRU_WIDTH
IN_SIZES = (RET_HEADS * RET_DK, RET_HEADS * RET_DK, RET_WIDTH, RET_WIDTH, LRU_WIDTH, LRU_WIDTH)
IN_COLS = sum(IN_SIZES)
IN_SPLITS = tuple(sum(IN_SIZES[:i + 1]) for i in range(len(IN_SIZES) - 1))
POOL_WINDOWS = (2, 4, 8, 16)
POOL_GROUP = D_MODEL // len(POOL_WINDOWS)
POOL_BUF = max(POOL_WINDOWS) - 1
D_FF = 2816
LN_EPS = 1e-5
DN_ALPHA = (2.0 * DEPTH) ** 0.25
DN_BETA = (8.0 * DEPTH) ** -0.25

kernel_name = 'hybrid_retention_rglru_pool_step'


def layer_norm(x, g, b):
    xf = x.astype(jnp.float32)
    mu = jnp.mean(xf, axis=-1, keepdims=True)
    var = jnp.mean(jnp.square(xf - mu), axis=-1, keepdims=True)
    return ((xf - mu) * lax.rsqrt(var + LN_EPS) * g + b).astype(x.dtype)


def swiglu_half(x, wg, wu, wd):
    return 0.5 * ((jax.nn.silu(x @ wg) * (x @ wu)) @ wd)


def rotary(x, pos):
    half = x.shape[-1] // 2
    inv = ROPE_BASE ** (-jnp.arange(half, dtype=jnp.float32) / half)
    ang = pos[:, None] * inv[None, :]
    cos = jnp.cos(ang)[None, :, None, :]
    sin = jnp.sin(ang)[None, :, None, :]
    x1, x2 = x[..., :half], x[..., half:]
    return jnp.concatenate([x1 * cos - x2 * sin, x2 * cos + x1 * sin], axis=-1)


def retention(q, k, v, s0):
    b_, t_ = q.shape[:2]
    c = RET_CHUNK if t_ % RET_CHUNK == 0 else t_
    nc = t_ // c
    lg = jnp.log1p(-jnp.exp2(-5.0 - jnp.arange(RET_HEADS, dtype=jnp.float32)))
    idx = jnp.arange(c, dtype=jnp.float32)
    diff = idx[:, None] - idx[None, :]
    decay = jnp.where(diff >= 0, jnp.exp(lg[:, None, None] * jnp.maximum(diff, 0.0)), 0.0)
    q_dec = jnp.exp(lg[:, None] * (idx + 1.0))[..., None]
    k_dec = jnp.exp(lg[:, None] * (c - 1.0 - idx))[..., None]
    c_dec = jnp.exp(lg * c)[:, None, None]

    def to_chunks(t):
        return t.reshape(b_, nc, c, RET_HEADS, t.shape[-1]).transpose(1, 0, 3, 2, 4)

    def step(s, qkv):
        qc, kc, vc = qkv
        scores = jnp.einsum('bhid,bhjd->bhij', qc, kc) * decay
        o = jnp.einsum('bhij,bhjv->bhiv', scores, vc) + jnp.einsum('bhid,bhdv->bhiv', qc * q_dec, s)
        s = s * c_dec + jnp.einsum('bhjd,bhjv->bhdv', kc * k_dec, vc)
        return s, o

    s_fin, o = lax.scan(step, s0, (to_chunks(q), to_chunks(k), to_chunks(v)))
    o = o.transpose(1, 0, 3, 2, 4).reshape(b_, t_, RET_HEADS, RET_DV)
    return o, s_fin


def causal_dwconv(u, buf, w, b):
    t_ = u.shape[1]
    ext = jnp.concatenate([buf.astype(u.dtype), u], axis=1)
    y = b + sum(ext[:, i:i + t_] * w[i] for i in range(CONV_W))
    return y, ext[:, -(CONV_W - 1):]


def rg_lru(u, h0, wa, ba, wi, bi, lam):
    b_, t_, _ = u.shape
    uf = u.astype(jnp.float32)
    ub = uf.reshape(b_, t_, LRU_BLOCKS, LRU_BLOCK)
    r = jax.nn.sigmoid(jnp.einsum('btnc,ncd->btnd', ub, wa).reshape(b_, t_, LRU_WIDTH) + ba)
    i = jax.nn.sigmoid(jnp.einsum('btnc,ncd->btnd', ub, wi).reshape(b_, t_, LRU_WIDTH) + bi)
    log_a = -LRU_C * jax.nn.softplus(-lam.astype(jnp.float32)) * r
    a = jnp.exp(log_a)
    xin = jnp.sqrt(-jnp.expm1(2.0 * log_a)) * (i * uf)

    def combine(lhs, rhs):
        a1, b1 = lhs
        a2, b2 = rhs
        return a1 * a2, a2 * b1 + b2

    a_cum, b_cum = lax.associative_scan(combine, (a, xin), axis=1)
    h = a_cum * h0.astype(jnp.float32)[:, None, :] + b_cum
    return h, h[:, -1]


def mix_ab(x, pos, s_ret, h0, conv_buf, w_in, w_out, gn_g, conv_w, conv_b, wa, ba, wi, bi, lam):
    b_, t_, _ = x.shape
    proj = x @ w_in
    q, k, v, g, ux, ug = jnp.split(proj, IN_SPLITS, axis=-1)
    q = rotary(q.reshape(b_, t_, RET_HEADS, RET_DK).astype(jnp.float32), pos) * (RET_DK ** -0.5)
    k = rotary(k.reshape(b_, t_, RET_HEADS, RET_DK).astype(jnp.float32), pos)
    v = v.reshape(b_, t_, RET_HEADS, RET_DV).astype(jnp.float32)
    o, s_new = retention(q, k, v, s_ret.astype(jnp.float32))
    mu = jnp.mean(o, axis=-1, keepdims=True)
    var = jnp.mean(jnp.square(o - mu), axis=-1, keepdims=True)
    o = ((o - mu) * lax.rsqrt(var + LN_EPS)).reshape(b_, t_, RET_WIDTH) * gn_g
    ret_out = (jax.nn.silu(g.astype(jnp.float32)) * o).astype(x.dtype)
    uc, conv_new = causal_dwconv(ux, conv_buf, conv_w, conv_b)
    h, h_new = rg_lru(uc, h0, wa, ba, wi, bi, lam)
    lru_out = (h * jax.nn.gelu(ug.astype(jnp.float32))).astype(x.dtype)
    y = jnp.concatenate([ret_out, lru_out], axis=-1) @ w_out
    return y, s_new.astype(x.dtype), h_new.astype(x.dtype), conv_new.astype(x.dtype)


def pool_mix(x, pos0, buf, w, bias, scale):
    b_, t_, d_ = x.shape
    ext = jnp.concatenate([buf.astype(x.dtype), x], axis=1)
    cs = jnp.cumsum(ext.astype(jnp.float32), axis=1)
    cs = jnp.concatenate([jnp.zeros((b_, 1, d_), jnp.float32), cs], axis=1)
    tpos = jnp.arange(t_, dtype=jnp.float32) + pos0
    xf = x.astype(jnp.float32)
    outs = []
    for gi, wnd in enumerate(POOL_WINDOWS):
        lo, hi = gi * POOL_GROUP, (gi + 1) * POOL_GROUP
        c = cs[:, :, lo:hi]
        s = c[:, POOL_BUF + 1:] - c[:, POOL_BUF + 1 - wnd:POOL_BUF + 1 - wnd + t_]
        cnt = jnp.minimum(float(wnd), tpos + 1.0)
        dlt = s / cnt[None, :, None] - xf[:, :, lo:hi]
        outs.append(jnp.einsum('btc,cd->btd', dlt, w[gi]))
    y = (jnp.concatenate(outs, axis=-1) + bias) * scale
    return y.astype(x.dtype), ext[:, -POOL_BUF:]


def trunk(x, pos0, ret0, h0, conv0, pool0, w_ffn_gate, w_ffn_up, w_ffn_down, ln_g, ln_b,
          w_mix_in, w_mix_out, ret_gn_g, lru_conv_w, lru_conv_b, lru_wa, lru_ba, lru_wi, lru_bi,
          lru_lambda, pool_w, pool_b, pool_scale):
    t_ = x.shape[1]
    pos = jnp.arange(t_, dtype=jnp.float32) + pos0
    rets, hs, convs, pools = [], [], [], []
    for layer in range(DEPTH):
        j = layer // 2
        x = layer_norm(DN_ALPHA * x + swiglu_half(x, w_ffn_gate[layer, 0], w_ffn_up[layer, 0], w_ffn_down[layer, 0]),
                       ln_g[layer, 0], ln_b[layer, 0])
        if layer % 2 == 0:
            y, s_new, h_new, c_new = mix_ab(x, pos, ret0[j], h0[j], conv0[j], w_mix_in[j], w_mix_out[j],
                                            ret_gn_g[j], lru_conv_w[j], lru_conv_b[j], lru_wa[j], lru_ba[j],
                                            lru_wi[j], lru_bi[j], lru_lambda[j])
            rets.append(s_new)
            hs.append(h_new)
            convs.append(c_new)
        else:
            y, p_new = pool_mix(x, pos0, pool0[j], pool_w[j], pool_b[j], pool_scale[j])
            pools.append(p_new)
        x = layer_norm(DN_ALPHA * x + y, ln_g[layer, 1], ln_b[layer, 1])
        x = layer_norm(DN_ALPHA * x + swiglu_half(x, w_ffn_gate[layer, 1], w_ffn_up[layer, 1], w_ffn_down[layer, 1]),
                       ln_g[layer, 2], ln_b[layer, 2])
    return x, jnp.stack(rets), jnp.stack(hs), jnp.stack(convs), jnp.stack(pools)


def setup_inputs(seed: int = 0) -> dict:
    key = jax.random.key(seed)
    ks = jax.random.split(key, 24)
    f32 = jnp.float32
    nrm = lambda k, s: jax.random.normal(k, s, f32)
    col_scale = jnp.concatenate([
        jnp.ones((2 * RET_HEADS * RET_DK,), f32), jnp.full((RET_WIDTH,), DN_BETA, f32),
        jnp.ones((RET_WIDTH,), f32), jnp.full((LRU_WIDTH,), DN_BETA, f32), jnp.ones((LRU_WIDTH,), f32)])
    a_target = jax.random.uniform(ks[20], (N_A_LAYERS, LRU_WIDTH), f32, 0.9, 0.999)
    sig = a_target ** (1.0 / LRU_C)
    return {
        'x_prompt': nrm(ks[0], (BATCH, SEQ, D_MODEL)),
        'x_sample': nrm(ks[1], (DEC_BATCH, DEC_SEQ, D_MODEL)),
        'state_ret': 0.5 * nrm(ks[2], (N_A_LAYERS, DEC_BATCH, RET_HEADS, RET_DK, RET_DV)),
        'state_lru_h': 0.5 * nrm(ks[3], (N_A_LAYERS, DEC_BATCH, LRU_WIDTH)),
        'state_lru_conv': nrm(ks[4], (N_A_LAYERS, DEC_BATCH, CONV_W - 1, LRU_WIDTH)),
        'state_pool': nrm(ks[5], (N_C_LAYERS, DEC_BATCH, POOL_BUF, D_MODEL)),
        'w_ffn_gate': nrm(ks[6], (DEPTH, 2, D_MODEL, D_FF)) * D_MODEL ** -0.5,
        'w_ffn_up': nrm(ks[7], (DEPTH, 2, D_MODEL, D_FF)) * (D_MODEL ** -0.5 * DN_BETA),
        'w_ffn_down': nrm(ks[8], (DEPTH, 2, D_FF, D_MODEL)) * (D_FF ** -0.5 * DN_BETA),
        'ln_g': 1.0 + 0.02 * nrm(ks[9], (DEPTH, 3, D_MODEL)),
        'ln_b': 0.02 * nrm(ks[10], (DEPTH, 3, D_MODEL)),
        'w_mix_in': nrm(ks[11], (N_A_LAYERS, D_MODEL, IN_COLS)) * D_MODEL ** -0.5 * col_scale,
        'w_mix_out': nrm(ks[12], (N_A_LAYERS, MIX_WIDTH, D_MODEL)) * (MIX_WIDTH ** -0.5 * DN_BETA),
        'ret_gn_g': 1.0 + 0.02 * nrm(ks[13], (N_A_LAYERS, RET_WIDTH)),
        'lru_conv_w': nrm(ks[14], (N_A_LAYERS, CONV_W, LRU_WIDTH)) * CONV_W ** -0.5,
        'lru_conv_b': 0.02 * nrm(ks[15], (N_A_LAYERS, LRU_WIDTH)),
        'lru_wa': nrm(ks[16], (N_A_LAYERS, LRU_BLOCKS, LRU_BLOCK, LRU_BLOCK)) * LRU_BLOCK ** -0.5,
        'lru_ba': 0.02 * nrm(ks[17], (N_A_LAYERS, LRU_WIDTH)),
        'lru_wi': nrm(ks[18], (N_A_LAYERS, LRU_BLOCKS, LRU_BLOCK, LRU_BLOCK)) * LRU_BLOCK ** -0.5,
        'lru_bi': 0.02 * nrm(ks[19], (N_A_LAYERS, LRU_WIDTH)),
        'lru_lambda': jnp.log(sig) - jnp.log1p(-sig),
        'pool_w': nrm(ks[21], (N_C_LAYERS, len(POOL_WINDOWS), POOL_GROUP, POOL_GROUP)) * (POOL_GROUP ** -0.5 * DN_BETA),
        'pool_b': 0.02 * nrm(ks[22], (N_C_LAYERS, D_MODEL)),
        'pool_scale': 1.0 + 0.1 * nrm(ks[23], (N_C_LAYERS, D_MODEL)),
    }


def reference(x_prompt, x_sample, state_ret, state_lru_h, state_lru_conv, state_pool,
              w_ffn_gate, w_ffn_up, w_ffn_down, ln_g, ln_b, w_mix_in, w_mix_out, ret_gn_g,
              lru_conv_w, lru_conv_b, lru_wa, lru_ba, lru_wi, lru_bi, lru_lambda,
              pool_w, pool_b, pool_scale):
    bp = x_prompt.shape[0]
    dt = x_prompt.dtype
    ret0_p = jnp.zeros((N_A_LAYERS, bp, RET_HEADS, RET_DK, RET_DV), dt)
    h0_p = jnp.zeros((N_A_LAYERS, bp, LRU_WIDTH), dt)
    conv0_p = jnp.zeros((N_A_LAYERS, bp, CONV_W - 1, LRU_WIDTH), dt)
    pool0_p = jnp.zeros((N_C_LAYERS, bp, POOL_BUF, D_MODEL), dt)
    y_prompt, ret_p, h_p, conv_p, pool_p = trunk(
        x_prompt, 0, ret0_p, h0_p, conv0_p, pool0_p, w_ffn_gate, w_ffn_up, w_ffn_down, ln_g, ln_b,
        w_mix_in, w_mix_out, ret_gn_g, lru_conv_w, lru_conv_b, lru_wa, lru_ba, lru_wi, lru_bi,
        lru_lambda, pool_w, pool_b, pool_scale)
    y_sample, ret_s, h_s, conv_s, pool_s = trunk(
        x_sample, PAST_LEN, state_ret, state_lru_h, state_lru_conv, state_pool, w_ffn_gate, w_ffn_up,
        w_ffn_down, ln_g, ln_b, w_mix_in, w_mix_out, ret_gn_g, lru_conv_w, lru_conv_b, lru_wa, lru_ba,
        lru_wi, lru_bi, lru_lambda, pool_w, pool_b, pool_scale)
    return (y_prompt, y_sample, ret_p, h_p, conv_p, pool_p, ret_s, h_s, conv_s, pool_s)
```

```python
import functools
import math

import numpy as np
import jax
import jax.numpy as jnp
from jax import lax
from jax.experimental import pallas as pl
from jax.experimental.pallas import tpu as pltpu

F32 = jnp.float32
BF16 = jnp.bfloat16

D_MODEL = 1024
DEPTH = 4
PAST_LEN = 16384
RET_HEADS = 4
HEAD_DIM = 128
RET_WIDTH = RET_HEADS * HEAD_DIM
RET_CHUNK = 128
ROPE_BASE = 10000.0
LRU_WIDTH = 512
LRU_C = 8.0
CONV_W = 4
POOL_WINDOWS = (2, 4, 8, 16)
POOL_GROUP = D_MODEL // len(POOL_WINDOWS)
POOL_BUF = max(POOL_WINDOWS) - 1
LN_EPS = 1e-5
DN_ALPHA = (2.0 * DEPTH) ** 0.25

SUBLANES = 8
ROW_TILE = 512
SEQ_TILE = 512
SAMPLE_GROUP = 16
VMEM_LIMIT = 56 * 1024 * 1024


def _sigmoid(x):
    return 1.0 / (1.0 + jnp.exp(-x))


def _silu(x):
    return x * _sigmoid(x)


def _gelu_tanh(x):
    return x * (0.5 * (1.0 + jnp.tanh(math.sqrt(2.0 / math.pi) * (x + 0.044715 * (x * x * x)))))


def _layer_norm(z, g, b):
    mu = jnp.mean(z, axis=-1, keepdims=True)
    d = z - mu
    var = jnp.mean(d * d, axis=-1, keepdims=True)
    return d * lax.rsqrt(var + LN_EPS) * g + b


def _unit_norm(z):
    mu = jnp.mean(z, axis=-1, keepdims=True)
    d = z - mu
    var = jnp.mean(d * d, axis=-1, keepdims=True)
    return d * lax.rsqrt(var + LN_EPS)


def _dot(a, b):
    return jnp.dot(a, b, preferred_element_type=F32)


def _resident(arr):
    nd = arr.ndim
    return pl.BlockSpec(arr.shape, lambda *_: (0,) * nd, pipeline_mode=pl.Buffered(1))


def _tokenwise(body, rows_p, rows_s, consts, outs, row_maps=None):
    n_rows_p = max(a.shape[0] for a in rows_p)
    n_rows_s = rows_s[0].shape[0]
    assert n_rows_p % ROW_TILE == 0
    n_p = n_rows_p // ROW_TILE
    n_in = len(rows_p)
    n_c = len(consts)
    n_o = len(outs)
    row_maps = row_maps or [None] * n_in

    def kern(*refs):
        p_refs = refs[:n_in]
        s_refs = refs[n_in:2 * n_in]
        c_refs = refs[2 * n_in:2 * n_in + n_c]
        op_refs = refs[2 * n_in + n_c:2 * n_in + n_c + n_o]
        os_refs = refs[2 * n_in + n_c + n_o:]
        i = pl.program_id(0)

        @pl.when(i < n_p)
        def _():
            for o, v in zip(op_refs, body([r[...] for r in p_refs], c_refs)):
                o[...] = v.astype(o.dtype)

        @pl.when(i == n_p)
        def _():
            for o, v in zip(os_refs, body([r[...] for r in s_refs], c_refs)):
                o[...] = v.astype(o.dtype)

    def p_spec(cols, wrap):
        if wrap is None:
            return pl.BlockSpec((ROW_TILE, cols), lambda i: (jnp.minimum(i, n_p - 1), 0))
        return pl.BlockSpec((ROW_TILE, cols), lambda i: (jnp.minimum(i, n_p - 1) % wrap, 0))

    in_specs = ([p_spec(a.shape[1], w) for a, w in zip(rows_p, row_maps)]
                + [pl.BlockSpec(a.shape, lambda i: (0, 0)) for a in rows_s]
                + [_resident(c) for c in consts])
    out_specs = ([p_spec(c, None) for c, _ in outs]
                 + [pl.BlockSpec((n_rows_s, c), lambda i: (0, 0)) for c, _ in outs])
    out_shape = ([jax.ShapeDtypeStruct((n_rows_p, c), dt) for c, dt in outs]
                 + [jax.ShapeDtypeStruct((n_rows_s, c), dt) for c, dt in outs])
    res = pl.pallas_call(
        kern, grid=(n_p + 1,), in_specs=in_specs, out_specs=out_specs, out_shape=out_shape,
        compiler_params=pltpu.CompilerParams(dimension_semantics=("arbitrary",),
                                             vmem_limit_bytes=VMEM_LIMIT),
    )(*rows_p, *rows_s, *consts)
    return res[:n_o], res[n_o:]


def _ffn_ln_body(rows, consts):
    (x,) = rows
    wg, wu, wd, g, b = consts
    xb = x.astype(BF16)
    hg = _dot(xb, wg[...])
    hu = _dot(xb, wu[...])
    act = (_silu(hg) * hu).astype(BF16)
    y = _dot(act, wd[...])
    return [_layer_norm(DN_ALPHA * x + 0.5 * y, g[...], b[...])]


def _mix_in_body(rows, consts):
    x, cos, sin = rows
    (w,) = consts
    proj = _dot(x.astype(BF16), w[...])
    pieces = []
    for j in range(2 * RET_HEADS):
        blk = proj[:, j * HEAD_DIM:(j + 1) * HEAD_DIM]
        rot = blk * cos + pltpu.roll(blk, HEAD_DIM // 2, 1) * sin
        if j < RET_HEADS:
            rot = rot * (HEAD_DIM ** -0.5)
        pieces.append(rot)
    pieces.append(proj[:, 2 * RET_WIDTH:])
    return [jnp.concatenate(pieces, axis=1)]


def _mix_out_ln_body(rows, consts):
    x, ret, lru = rows
    w_ret, w_lru, g, b = consts
    y = _dot(ret, w_ret[...]) + _dot(lru, w_lru[...])
    return [_layer_norm(DN_ALPHA * x + y, g[...], b[...])]


def _retention_tables():
    lg = np.log1p(-np.exp2(-5.0 - np.arange(RET_HEADS, dtype=np.float64)))
    idx = np.arange(RET_CHUNK, dtype=np.float64)
    diff = idx[:, None] - idx[None, :]
    decay = np.where(diff >= 0, np.exp(lg[:, None, None] * np.maximum(diff, 0.0)), 0.0)
    ones = np.ones((1, 1, HEAD_DIM))
    q_dec = np.exp(lg[:, None] * (idx + 1.0))[..., None] * ones
    k_dec = np.exp(lg[:, None] * (RET_CHUNK - 1.0 - idx))[..., None] * ones
    c_dec = np.exp(lg * RET_CHUNK)
    as32 = lambda a: jnp.asarray(a.astype(np.float32))
    return as32(decay), as32(q_dec), as32(k_dec), [float(c) for c in c_dec], [float(g) for g in np.exp(lg)]


def _ret_prompt_kernel(c_dec, n_t, q_ref, k_ref, v_ref, g_ref, gn_ref, dec_ref, qd_ref, kd_ref,
                       o_ref, s_ref, state):
    t = pl.program_id(1)

    @pl.when(t == 0)
    def _():
        state[...] = jnp.zeros_like(state)

    for h in range(RET_HEADS):
        cols = slice(h * HEAD_DIM, (h + 1) * HEAD_DIM)
        s = state[h]
        for c in range(SEQ_TILE // RET_CHUNK):
            rows = slice(c * RET_CHUNK, (c + 1) * RET_CHUNK)
            qc = q_ref[rows, cols]
            kc = k_ref[rows, cols]
            vb = v_ref[rows, cols].astype(BF16)
            scores = lax.dot_general(qc.astype(BF16), kc.astype(BF16), (((1,), (1,)), ((), ())),
                                     preferred_element_type=F32) * dec_ref[h]
            o = _dot(scores.astype(BF16), vb) + _dot((qc * qd_ref[h]).astype(BF16), s.astype(BF16))
            s = s * c_dec[h] + _dot((kc * kd_ref[h]).T.astype(BF16), vb)
            o = _unit_norm(o) * gn_ref[:, cols]
            o_ref[rows, cols] = (_silu(g_ref[rows, cols]) * o).astype(o_ref.dtype)
        state[h] = s

    @pl.when(t == n_t - 1)
    def _():
        s_ref[0] = state[...]


def _ret_prompt(proj, gn_g, batch, seq):
    decay, q_dec, k_dec, c_dec, _ = _retention_tables()
    n_t = seq // SEQ_TILE
    col = lambda j: pl.BlockSpec((SEQ_TILE, RET_WIDTH), lambda b, t: (b * n_t + t, j))
    return pl.pallas_call(
        functools.partial(_ret_prompt_kernel, c_dec, n_t),
        grid=(batch, n_t),
        in_specs=[col(0), col(1), col(2), col(3), _resident(gn_g), _resident(decay), _resident(q_dec),
                  _resident(k_dec)],
        out_specs=[pl.BlockSpec((SEQ_TILE, RET_WIDTH), lambda b, t: (b * n_t + t, 0)),
                   pl.BlockSpec((1, RET_HEADS, HEAD_DIM, HEAD_DIM), lambda b, t: (b, 0, 0, 0))],
        out_shape=[jax.ShapeDtypeStruct((batch * seq, RET_WIDTH), BF16),
                   jax.ShapeDtypeStruct((batch, RET_HEADS, HEAD_DIM, HEAD_DIM), F32)],
        scratch_shapes=[pltpu.VMEM((RET_HEADS, HEAD_DIM, HEAD_DIM), F32)],
        compiler_params=pltpu.CompilerParams(dimension_semantics=("parallel", "arbitrary"),
                                             vmem_limit_bytes=VMEM_LIMIT),
    )(proj, proj, proj, proj, gn_g, decay, q_dec, k_dec)


def _ret_sample_kernel(gamma, q_ref, k_ref, v_ref, g_ref, gn_ref, s_in, o_ref, s_out):
    n = SAMPLE_GROUP
    row = lax.broadcasted_iota(jnp.int32, (n, n * HEAD_DIM), 0)
    blk = lax.broadcasted_iota(jnp.int32, (n, n * HEAD_DIM), 1) // HEAD_DIM
    own = row == blk
    for h in range(RET_HEADS):
        cols = slice(h * HEAD_DIM, (h + 1) * HEAD_DIM)
        q = q_ref[:, cols]
        k = k_ref[:, cols]
        v = v_ref[:, cols]
        v_wide = jnp.where(own, jnp.concatenate([v] * n, axis=1), 0.0).astype(BF16)
        outer = _dot(k.T.astype(BF16), v_wide)
        new = []
        for r in range(n):
            s_new = s_in[r, h] * gamma[h] + outer[:, r * HEAD_DIM:(r + 1) * HEAD_DIM]
            s_out[r, h] = s_new
            new.append(s_new.astype(BF16))
        wide = _dot(q.astype(BF16), jnp.concatenate(new, axis=1))
        wide = jnp.where(own, wide, 0.0)
        o = wide[:, :HEAD_DIM]
        for r in range(1, n):
            o = o + wide[:, r * HEAD_DIM:(r + 1) * HEAD_DIM]
        o = _unit_norm(o) * gn_ref[:, cols]
        o_ref[:, cols] = (_silu(g_ref[:, cols]) * o).astype(o_ref.dtype)


def _ret_sample(proj, gn_g, s0):
    _, _, _, _, gamma = _retention_tables()
    n_rows = proj.shape[0]
    n = SAMPLE_GROUP
    col = lambda j: pl.BlockSpec((n, RET_WIDTH), lambda i: (i, j))
    s_spec = pl.BlockSpec((n, RET_HEADS, HEAD_DIM, HEAD_DIM), lambda i: (i, 0, 0, 0))
    return pl.pallas_call(
        functools.partial(_ret_sample_kernel, gamma),
        grid=(n_rows // n,),
        in_specs=[col(0), col(1), col(2), col(3), _resident(gn_g), s_spec],
        out_specs=[pl.BlockSpec((n, RET_WIDTH), lambda i: (i, 0)), s_spec],
        out_shape=[jax.ShapeDtypeStruct((n_rows, RET_WIDTH), BF16),
                   jax.ShapeDtypeStruct(s0.shape, F32)],
        compiler_params=pltpu.CompilerParams(dimension_semantics=("parallel",),
                                             vmem_limit_bytes=VMEM_LIMIT),
    )(proj, proj, proj, proj, gn_g, s0)


def _lru_gates(uc, w_gates, b_gates, lam):
    pre = _dot(uc.astype(BF16), w_gates) + b_gates
    r = _sigmoid(pre[:, :LRU_WIDTH])
    i = _sigmoid(pre[:, LRU_WIDTH:])
    neg = -lam
    softplus = jnp.maximum(neg, 0.0) + jnp.log1p(jnp.exp(-jnp.abs(neg)))
    log_a = (-LRU_C * softplus) * r
    a = jnp.exp(log_a)
    xin = jnp.sqrt(jnp.tanh(-log_a) * (1.0 + a * a)) * (i * uc)
    return a, xin


def _lru_prompt_kernel(ux_ref, ug_ref, cw_ref, cb_ref, wg_ref, bg_ref, lam_ref,
                       o_ref, h_ref, tail_ref, ext, a_scr, x_scr, h_scr, carry):
    t = pl.program_id(1)
    n = SEQ_TILE
    pad = SUBLANES

    @pl.when(t == 0)
    def _():
        ext[0:pad, :] = jnp.zeros((pad, LRU_WIDTH), F32)
        carry[...] = jnp.zeros_like(carry)

    u = ux_ref[...]
    ext[pad:pad + n, :] = u
    uc = cb_ref[...] + cw_ref[CONV_W - 1:CONV_W, :] * u
    for j in range(1, CONV_W):
        uc = uc + cw_ref[CONV_W - 1 - j:CONV_W - j, :] * ext[pad - j:pad - j + n, :]
    ext[0:pad, :] = u[n - pad:, :]
    tail_ref[0] = u[n - pad:, :]

    a, x = _lru_gates(uc, wg_ref[...], bg_ref[...], lam_ref[...])

    step = lax.broadcasted_iota(jnp.int32, (n, LRU_WIDTH), 0) % SUBLANES
    shift = 1
    while shift < SUBLANES:
        inside = step >= shift
        x = a * jnp.where(inside, pltpu.roll(x, shift, 0), 0.0) + x
        a = a * jnp.where(inside, pltpu.roll(a, shift, 0), 1.0)
        shift *= 2
    a_scr[...] = a
    x_scr[...] = x

    def group(i, h):
        r = pl.ds(pl.multiple_of(i * SUBLANES, SUBLANES), SUBLANES)
        hg = a_scr[r, :] * h + x_scr[r, :]
        h_scr[r, :] = hg
        return jnp.broadcast_to(hg[SUBLANES - 1:SUBLANES, :], (SUBLANES, LRU_WIDTH))

    h_last = lax.fori_loop(0, n // SUBLANES, group, carry[...])
    carry[...] = h_last
    h_ref[0] = h_last
    o_ref[...] = (h_scr[...] * _gelu_tanh(ug_ref[...])).astype(o_ref.dtype)


def _lru_prompt(proj, conv_w, conv_b, w_gates, b_gates, lam, batch, seq):
    n_t = seq // SEQ_TILE
    ux_col = 2 * RET_WIDTH * 2 // LRU_WIDTH
    col = lambda j: pl.BlockSpec((SEQ_TILE, LRU_WIDTH), lambda b, t: (b * n_t + t, j))
    small = pl.BlockSpec((1, SUBLANES, LRU_WIDTH), lambda b, t: (b, 0, 0))
    consts = [conv_w, conv_b, w_gates, b_gates, lam]
    return pl.pallas_call(
        _lru_prompt_kernel,
        grid=(batch, n_t),
        in_specs=[col(ux_col), col(ux_col + 1)] + [_resident(c) for c in consts],
        out_specs=[pl.BlockSpec((SEQ_TILE, LRU_WIDTH), lambda b, t: (b * n_t + t, 0)), small, small],
        out_shape=[jax.ShapeDtypeStruct((batch * seq, LRU_WIDTH), BF16),
                   jax.ShapeDtypeStruct((batch, SUBLANES, LRU_WIDTH), F32),
                   jax.ShapeDtypeStruct((batch, SUBLANES, LRU_WIDTH), F32)],
        scratch_shapes=[pltpu.VMEM((SEQ_TILE + SUBLANES, LRU_WIDTH), F32),
                        pltpu.VMEM((SEQ_TILE, LRU_WIDTH), F32),
                        pltpu.VMEM((SEQ_TILE, LRU_WIDTH), F32),
                        pltpu.VMEM((SEQ_TILE, LRU_WIDTH), F32),
                        pltpu.VMEM((SUBLANES, LRU_WIDTH), F32)],
        compiler_params=pltpu.CompilerParams(dimension_semantics=("parallel", "arbitrary"),
                                             vmem_limit_bytes=VMEM_LIMIT),
    )(proj, proj, *consts)


def _lru_sample_kernel(ux_ref, ug_ref, b0_ref, b1_ref, b2_ref, h0_ref, cw_ref, cb_ref, wg_ref, bg_ref,
                       lam_ref, o_ref, h_ref):
    u = ux_ref[...]
    uc = (cb_ref[...] + cw_ref[3:4, :] * u + cw_ref[2:3, :] * b2_ref[...]
          + cw_ref[1:2, :] * b1_ref[...] + cw_ref[0:1, :] * b0_ref[...])
    a, x = _lru_gates(uc, wg_ref[...], bg_ref[...], lam_ref[...])
    h = a * h0_ref[...] + x
    h_ref[...] = h
    o_ref[...] = (h * _gelu_tanh(ug_ref[...])).astype(o_ref.dtype)


def _lru_sample(proj, conv_buf, h0, conv_w, conv_b, w_gates, b_gates, lam):
    n_rows = proj.shape[0]
    ux_col = 2 * RET_WIDTH * 2 // LRU_WIDTH
    col = lambda j: pl.BlockSpec((n_rows, LRU_WIDTH), lambda i: (0, j))
    bufs = [conv_buf[:, j, :] for j in range(CONV_W - 1)]
    rest = bufs + [h0, conv_w, conv_b, w_gates, b_gates, lam]
    return pl.pallas_call(
        _lru_sample_kernel,
        grid=(1,),
        in_specs=[col(ux_col), col(ux_col + 1)] + [_resident(c) for c in rest],
        out_specs=[pl.BlockSpec((n_rows, LRU_WIDTH), lambda i: (0, 0))] * 2,
        out_shape=[jax.ShapeDtypeStruct((n_rows, LRU_WIDTH), BF16),
                   jax.ShapeDtypeStruct((n_rows, LRU_WIDTH), F32)],
        compiler_params=pltpu.CompilerParams(vmem_limit_bytes=VMEM_LIMIT),
    )(proj, proj, *rest)


def _pool_finish(x, deltas, w_ref, pb_ref, ps_ref, g_ref, b_ref):
    ys = [_dot(d.astype(BF16), w_ref[gi]) for gi, d in enumerate(deltas)]
    y = (jnp.concatenate(ys, axis=1) + pb_ref[...]) * ps_ref[...]
    return _layer_norm(DN_ALPHA * x + y, g_ref[...], b_ref[...])


def _pool_prompt_kernel(x_ref, w_ref, pb_ref, ps_ref, g_ref, b_ref, o_ref, ext):
    t = pl.program_id(1)
    n = SEQ_TILE
    halo = POOL_BUF + 1

    @pl.when(t == 0)
    def _():
        ext[0:halo, :] = jnp.zeros((halo, D_MODEL), F32)

    x = x_ref[...]
    ext[halo:halo + n, :] = x
    pos = t * n + lax.broadcasted_iota(jnp.int32, (n, POOL_GROUP), 0)
    deltas = []
    for gi, wnd in enumerate(POOL_WINDOWS):
        cols = slice(gi * POOL_GROUP, (gi + 1) * POOL_GROUP)
        w = ext[:, cols]
        shift = 1
        while shift < wnd:
            w = w + pltpu.roll(w, shift, 0)
            shift *= 2
        cnt = jnp.minimum(wnd, pos + 1).astype(F32)
        deltas.append(w[halo:, :] / cnt - x[:, cols])
    o_ref[...] = _pool_finish(x, deltas, w_ref, pb_ref, ps_ref, g_ref, b_ref)
    ext[0:halo, :] = x[n - halo:, :]


def _pool_prompt(x, consts, batch, seq):
    n_t = seq // SEQ_TILE
    spec = pl.BlockSpec((SEQ_TILE, D_MODEL), lambda b, t: (b * n_t + t, 0))
    return pl.pallas_call(
        _pool_prompt_kernel,
        grid=(batch, n_t),
        in_specs=[spec] + [_resident(c) for c in consts],
        out_specs=spec,
        out_shape=jax.ShapeDtypeStruct(x.shape, F32),
        scratch_shapes=[pltpu.VMEM((SEQ_TILE + POOL_BUF + 1, D_MODEL), F32)],
        compiler_params=pltpu.CompilerParams(dimension_semantics=("parallel", "arbitrary"),
                                             vmem_limit_bytes=VMEM_LIMIT),
    )(x, *consts)


def _pool_sample_kernel(x_ref, buf_ref, w_ref, pb_ref, ps_ref, g_ref, b_ref, o_ref):
    x = x_ref[...]
    deltas = []
    for gi, wnd in enumerate(POOL_WINDOWS):
        cols = slice(gi * POOL_GROUP, (gi + 1) * POOL_GROUP)
        s = x[:, cols]
        for j in range(1, wnd):
            s = s + buf_ref[POOL_BUF - j, :, cols]
        cnt = min(float(wnd), PAST_LEN + 1.0)
        deltas.append(s / cnt - x[:, cols])
    o_ref[...] = _pool_finish(x, deltas, w_ref, pb_ref, ps_ref, g_ref, b_ref)


def _pool_sample(x, buf_t, consts):
    full = lambda a: pl.BlockSpec(a.shape, lambda i: (0,) * a.ndim)
    return pl.pallas_call(
        _pool_sample_kernel,
        grid=(1,),
        in_specs=[full(x), full(buf_t)] + [_resident(c) for c in consts],
        out_specs=full(x),
        out_shape=jax.ShapeDtypeStruct(x.shape, F32),
        compiler_params=pltpu.CompilerParams(vmem_limit_bytes=VMEM_LIMIT),
    )(x, buf_t, *consts)


def _rotary_tables(positions):
    half = HEAD_DIM // 2
    inv = ROPE_BASE ** (-jnp.arange(half, dtype=F32) / half)
    ang = positions.astype(F32)[:, None] * inv[None, :]
    cos, sin = jnp.cos(ang), jnp.sin(ang)
    return jnp.concatenate([cos, cos], axis=1), jnp.concatenate([-sin, sin], axis=1)


def _block_diag(w):
    nb, n, _ = w.shape
    eye = jnp.eye(nb, dtype=w.dtype)
    return (eye[:, None, :, None] * w[:, :, None, :]).reshape(nb * n, nb * n)


def kernel(x_prompt, x_sample, state_ret, state_lru_h, state_lru_conv, state_pool, w_ffn_gate, w_ffn_up, w_ffn_down, ln_g, ln_b, w_mix_in, w_mix_out, ret_gn_g, lru_conv_w, lru_conv_b, lru_wa, lru_ba, lru_wi, lru_bi, lru_lambda, pool_w, pool_b, pool_scale):
    batch, seq, _ = x_prompt.shape
    n_s = x_sample.shape[0]
    assert x_sample.shape[1] == 1 and seq % SEQ_TILE == 0 and n_s % SAMPLE_GROUP == 0
    xp = x_prompt.reshape(batch * seq, D_MODEL)
    xs = x_sample.reshape(n_s, D_MODEL)
    row = lambda a: a.reshape(1, -1)

    cos_p, sin_p = _rotary_tables(jnp.arange(seq))
    cos_s, sin_s = _rotary_tables(jnp.full((n_s,), PAST_LEN))

    def ffn(xp, xs, layer, k):
        consts = [w_ffn_gate[layer, k].astype(BF16), w_ffn_up[layer, k].astype(BF16),
                  w_ffn_down[layer, k].astype(BF16), row(ln_g[layer, 2 * k]), row(ln_b[layer, 2 * k])]
        (op,), (os_,) = _tokenwise(_ffn_ln_body, [xp], [xs], consts, [(D_MODEL, F32)])
        return op, os_

    rets_p, hs_p, convs_p, pools_p = [], [], [], []
    rets_s, hs_s, convs_s, pools_s = [], [], [], []
    for layer in range(DEPTH):
        j = layer // 2
        xp, xs = ffn(xp, xs, layer, 0)
        if layer % 2 == 0:
            (proj_p,), (proj_s,) = _tokenwise(
                _mix_in_body, [xp, cos_p, sin_p], [xs, cos_s, sin_s], [w_mix_in[j].astype(BF16)],
                [(w_mix_in.shape[2], F32)], row_maps=[None, seq // ROW_TILE, seq // ROW_TILE])
            gn = row(ret_gn_g[j])
            w_gates = jnp.concatenate([_block_diag(lru_wa[j]), _block_diag(lru_wi[j])], axis=1).astype(BF16)
            b_gates = row(jnp.concatenate([lru_ba[j], lru_bi[j]]))
            lru_consts = (lru_conv_w[j], row(lru_conv_b[j]), w_gates, b_gates, row(lru_lambda[j]))

            ret_p, s_p = _ret_prompt(proj_p, gn, batch, seq)
            lru_p, h_p, tail_p = _lru_prompt(proj_p, *lru_consts, batch, seq)
            ret_s, s_s = _ret_sample(proj_s, gn, state_ret[j])
            lru_s, h_s = _lru_sample(proj_s, state_lru_conv[j], state_lru_h[j], *lru_consts)

            rets_p.append(s_p)
            hs_p.append(h_p[:, 0, :])
            convs_p.append(tail_p[:, SUBLANES - (CONV_W - 1):, :])
            rets_s.append(s_s)
            hs_s.append(h_s)
            ux_s = proj_s[:, 4 * RET_WIDTH:4 * RET_WIDTH + LRU_WIDTH]
            convs_s.append(jnp.concatenate([state_lru_conv[j][:, 1:, :], ux_s[:, None, :]], axis=1))

            w_out = w_mix_out[j].astype(BF16)
            consts = [w_out[:RET_WIDTH], w_out[RET_WIDTH:], row(ln_g[layer, 1]), row(ln_b[layer, 1])]
            (xp,), (xs,) = _tokenwise(_mix_out_ln_body, [xp, ret_p, lru_p], [xs, ret_s, lru_s], consts,
                                      [(D_MODEL, F32)])
        else:
            consts = [pool_w[j].astype(BF16), row(pool_b[j]), row(pool_scale[j]),
                      row(ln_g[layer, 1]), row(ln_b[layer, 1])]
            pools_p.append(xp.reshape(batch, seq, D_MODEL)[:, seq - POOL_BUF:, :])
            pools_s.append(jnp.concatenate([state_pool[j][:, 1:, :], xs[:, None, :]], axis=1))
            xp = _pool_prompt(xp, consts, batch, seq)
            xs = _pool_sample(xs, jnp.transpose(state_pool[j], (1, 0, 2)), consts)
        xp, xs = ffn(xp, xs, layer, 1)

    return (xp.reshape(batch, seq, D_MODEL), xs.reshape(n_s, 1, D_MODEL),
            jnp.stack(rets_p), jnp.stack(hs_p), jnp.stack(convs_p), jnp.stack(pools_p),
            jnp.stack(rets_s), jnp.stack(hs_s), jnp.stack(convs_s), jnp.stack(pools_s))
```

```python
import functools
import math

import numpy as np
import jax
import jax.numpy as jnp
from jax import lax
from jax.experimental import pallas as pl
from jax.experimental.pallas import tpu as pltpu

F32 = jnp.float32
BF16 = jnp.bfloat16

D_MODEL = 1024
DEPTH = 4
PAST_LEN = 16384
RET_HEADS = 4
HEAD_DIM = 128
RET_WIDTH = RET_HEADS * HEAD_DIM
RET_CHUNK = 128
ROPE_BASE = 10000.0
LRU_WIDTH = 512
LRU_C = 8.0
CONV_W = 4
POOL_WINDOWS = (2, 4, 8, 16)
POOL_GROUP = D_MODEL // len(POOL_WINDOWS)
POOL_BUF = max(POOL_WINDOWS) - 1
LN_EPS = 1e-5
DN_ALPHA = (2.0 * DEPTH) ** 0.25

SUBLANES = 8
ROW_TILE = 512
SEQ_TILE = 512
SAMPLE_GROUP = 16
VMEM_LIMIT = 56 * 1024 * 1024


def _sigmoid(x):
    return 0.5 * jnp.tanh(0.5 * x) + 0.5


def _silu(x):
    return x * _sigmoid(x)


def _gelu_tanh(x):
    return x * (0.5 * (1.0 + jnp.tanh(math.sqrt(2.0 / math.pi) * (x + 0.044715 * (x * x * x)))))


def _layer_norm(z, g, b):
    mu = jnp.mean(z, axis=-1, keepdims=True)
    d = z - mu
    var = jnp.mean(d * d, axis=-1, keepdims=True)
    return d * lax.rsqrt(var + LN_EPS) * g + b


def _unit_norm(z):
    mu = jnp.mean(z, axis=-1, keepdims=True)
    d = z - mu
    var = jnp.mean(d * d, axis=-1, keepdims=True)
    return d * lax.rsqrt(var + LN_EPS)


def _dot(a, b):
    return jnp.dot(a, b, preferred_element_type=F32)


def _resident(arr):
    nd = arr.ndim
    return pl.BlockSpec(arr.shape, lambda *_: (0,) * nd, pipeline_mode=pl.Buffered(1))


def _pick(arr, *lead, rows=None):
    tail = arr.shape[len(lead):]
    first = 0
    if rows is not None:
        tail = (rows[0],) + tail[1:]
        first = rows[1]
    index = tuple(lead) + (first,) + (0,) * (len(tail) - 1)
    return arr, pl.BlockSpec((None,) * len(lead) + tail, lambda *_: index, pipeline_mode=pl.Buffered(1))


def _tokenwise(body_p, body_s, rows_p, rows_s, consts, outs, row_maps=None, scratch=()):
    n_rows_p = max(a.shape[0] for a in rows_p)
    n_rows_s = rows_s[0].shape[0]
    assert n_rows_p % ROW_TILE == 0
    n_p = n_rows_p // ROW_TILE
    n_in_p, n_in_s, n_c, n_o = len(rows_p), len(rows_s), len(consts), len(outs)
    row_maps = row_maps or [None] * n_in_p
    consts = [c if isinstance(c, tuple) else (c, _resident(c)) for c in consts]

    def kern(*refs):
        refs = list(refs)
        p_refs, refs = refs[:n_in_p], refs[n_in_p:]
        s_refs, refs = refs[:n_in_s], refs[n_in_s:]
        c_refs, refs = refs[:n_c], refs[n_c:]
        op_refs, refs = refs[:n_o], refs[n_o:]
        os_refs, scr = refs[:n_o], refs[n_o:]
        i = pl.program_id(0)

        @pl.when(i < n_p)
        def _():
            for o, v in zip(op_refs, body_p(i, [r[...] for r in p_refs], c_refs, scr)):
                o[...] = v.astype(o.dtype)

        @pl.when(i == n_p)
        def _():
            for o, v in zip(os_refs, body_s([r[...] for r in s_refs], c_refs)):
                o[...] = v.astype(o.dtype)

    def p_spec(cols, wrap):
        if wrap is None:
            return pl.BlockSpec((ROW_TILE, cols), lambda i: (jnp.minimum(i, n_p - 1), 0))
        return pl.BlockSpec((ROW_TILE, cols), lambda i: (jnp.minimum(i, n_p - 1) % wrap, 0))

    in_specs = ([p_spec(a.shape[1], w) for a, w in zip(rows_p, row_maps)]
                + [pl.BlockSpec(a.shape, lambda i: (0, 0)) for a in rows_s]
                + [spec for _, spec in consts])
    out_specs = ([p_spec(c, None) for c, _ in outs]
                 + [pl.BlockSpec((n_rows_s, c), lambda i: (0, 0)) for c, _ in outs])
    out_shape = ([jax.ShapeDtypeStruct((n_rows_p, c), dt) for c, dt in outs]
                 + [jax.ShapeDtypeStruct((n_rows_s, c), dt) for c, dt in outs])
    res = pl.pallas_call(
        kern, grid=(n_p + 1,), in_specs=in_specs, out_specs=out_specs, out_shape=out_shape,
        scratch_shapes=list(scratch),
        compiler_params=pltpu.CompilerParams(dimension_semantics=("arbitrary",),
                                             vmem_limit_bytes=VMEM_LIMIT),
    )(*rows_p, *rows_s, *[a for a, _ in consts])
    return res[:n_o], res[n_o:]


def _same(body):
    return lambda step, rows, consts, scratch: body(rows, consts)


def _ffn_ln(x, consts):
    wg, wu, wd, g, b = consts
    xb = x.astype(BF16)
    hg = _dot(xb, wg[...])
    hu = _dot(xb, wu[...])
    act = (_silu(hg) * hu).astype(BF16)
    y = _dot(act, wd[...])
    return _layer_norm(DN_ALPHA * x + 0.5 * y, g[...], b[...])


def _ffn_ln_body(rows, consts):
    (x,) = rows
    return [_ffn_ln(x, consts)]


def _mix_in_body(rows, consts):
    x, cos, sin = rows
    (w,) = consts
    proj = _dot(x.astype(BF16), w[...])
    pieces = []
    for j in range(2 * RET_HEADS):
        blk = proj[:, j * HEAD_DIM:(j + 1) * HEAD_DIM]
        rot = blk * cos + pltpu.roll(blk, HEAD_DIM // 2, 1) * sin
        if j < RET_HEADS:
            rot = rot * (HEAD_DIM ** -0.5)
        pieces.append(rot)
    pieces.append(proj[:, 2 * RET_WIDTH:])
    return [jnp.concatenate(pieces, axis=1)]


def _mix_out_ffn_body(rows, consts):
    x, ret, lru = rows
    w_ret, w_lru, g, b = consts[:4]
    y = _dot(ret, w_ret[...]) + _dot(lru, w_lru[...])
    return [_ffn_ln(_layer_norm(DN_ALPHA * x + y, g[...], b[...]), consts[4:])]


def _retention_tables():
    lg = np.log1p(-np.exp2(-5.0 - np.arange(RET_HEADS, dtype=np.float64)))
    idx = np.arange(RET_CHUNK, dtype=np.float64)
    diff = idx[:, None] - idx[None, :]
    decay = np.where(diff >= 0, np.exp(lg[:, None, None] * np.maximum(diff, 0.0)), 0.0)
    ones = np.ones((1, 1, HEAD_DIM))
    q_dec = np.exp(lg[:, None] * (idx + 1.0))[..., None] * ones
    k_dec = np.exp(lg[:, None] * (RET_CHUNK - 1.0 - idx))[..., None] * ones
    c_dec = np.exp(lg * RET_CHUNK)
    as32 = lambda a: jnp.asarray(a.astype(np.float32))
    return as32(decay), as32(q_dec), as32(k_dec), [float(c) for c in c_dec], [float(g) for g in np.exp(lg)]


def _ret_prompt_kernel(c_dec, n_t, q_ref, k_ref, v_ref, g_ref, gn_ref, dec_ref, qd_ref, kd_ref,
                       o_ref, s_ref, state):
    t = pl.program_id(1)

    @pl.when(t == 0)
    def _():
        state[...] = jnp.zeros_like(state)

    for h in range(RET_HEADS):
        cols = slice(h * HEAD_DIM, (h + 1) * HEAD_DIM)
        s = state[h]
        for c in range(SEQ_TILE // RET_CHUNK):
            rows = slice(c * RET_CHUNK, (c + 1) * RET_CHUNK)
            qc = q_ref[rows, cols]
            kc = k_ref[rows, cols]
            vb = v_ref[rows, cols].astype(BF16)
            scores = lax.dot_general(qc.astype(BF16), kc.astype(BF16), (((1,), (1,)), ((), ())),
                                     preferred_element_type=F32) * dec_ref[h]
            o = _dot(scores.astype(BF16), vb) + _dot((qc * qd_ref[h]).astype(BF16), s.astype(BF16))
            s = s * c_dec[h] + _dot((kc * kd_ref[h]).T.astype(BF16), vb)
            o = _unit_norm(o) * gn_ref[:, cols]
            o_ref[rows, cols] = (_silu(g_ref[rows, cols]) * o).astype(o_ref.dtype)
        state[h] = s

    @pl.when(t == n_t - 1)
    def _():
        s_ref[0] = state[...]


def _ret_prompt(proj, gn_g, batch, seq):
    decay, q_dec, k_dec, c_dec, _ = _retention_tables()
    n_t = seq // SEQ_TILE
    col = lambda j: pl.BlockSpec((SEQ_TILE, RET_WIDTH), lambda b, t: (b * n_t + t, j))
    return pl.pallas_call(
        functools.partial(_ret_prompt_kernel, c_dec, n_t),
        grid=(batch, n_t),
        in_specs=[col(0), col(1), col(2), col(3), _resident(gn_g), _resident(decay), _resident(q_dec),
                  _resident(k_dec)],
        out_specs=[pl.BlockSpec((SEQ_TILE, RET_WIDTH), lambda b, t: (b * n_t + t, 0)),
                   pl.BlockSpec((1, RET_HEADS, HEAD_DIM, HEAD_DIM), lambda b, t: (b, 0, 0, 0))],
        out_shape=[jax.ShapeDtypeStruct((batch * seq, RET_WIDTH), BF16),
                   jax.ShapeDtypeStruct((batch, RET_HEADS, HEAD_DIM, HEAD_DIM), F32)],
        scratch_shapes=[pltpu.VMEM((RET_HEADS, HEAD_DIM, HEAD_DIM), F32)],
        compiler_params=pltpu.CompilerParams(dimension_semantics=("parallel", "arbitrary"),
                                             vmem_limit_bytes=VMEM_LIMIT),
    )(proj, proj, proj, proj, gn_g, decay, q_dec, k_dec)


def _ret_sample_kernel(gamma, q_ref, k_ref, v_ref, g_ref, gn_ref, s_in, *rest):
    o_ref, s_out = rest[-2:]
    n = SAMPLE_GROUP
    row = lax.broadcasted_iota(jnp.int32, (n, n * HEAD_DIM), 0)
    blk = lax.broadcasted_iota(jnp.int32, (n, n * HEAD_DIM), 1) // HEAD_DIM
    own = row == blk
    for h in range(RET_HEADS):
        cols = slice(h * HEAD_DIM, (h + 1) * HEAD_DIM)
        q = q_ref[:, cols]
        k = k_ref[:, cols]
        v = v_ref[:, cols]
        v_wide = jnp.where(own, jnp.concatenate([v] * n, axis=1), 0.0).astype(BF16)
        outer = _dot(k.T.astype(BF16), v_wide)
        new = []
        for r in range(n):
            s_new = s_in[r, h] * gamma[h] + outer[:, r * HEAD_DIM:(r + 1) * HEAD_DIM]
            s_out[r, h] = s_new
            new.append(s_new.astype(BF16))
        wide = _dot(q.astype(BF16), jnp.concatenate(new, axis=1))
        wide = jnp.where(own, wide, 0.0)
        o = wide[:, :HEAD_DIM]
        for r in range(1, n):
            o = o + wide[:, r * HEAD_DIM:(r + 1) * HEAD_DIM]
        o = _unit_norm(o) * gn_ref[:, cols]
        o_ref[:, cols] = (_silu(g_ref[:, cols]) * o).astype(o_ref.dtype)


def _ret_sample(proj, gn_g, states, j, new_states):
    _, _, _, _, gamma = _retention_tables()
    n_rows = proj.shape[0]
    n = SAMPLE_GROUP
    col = lambda c: pl.BlockSpec((n, RET_WIDTH), lambda i: (i, c))
    s_spec = pl.BlockSpec((None, n, RET_HEADS, HEAD_DIM, HEAD_DIM), lambda i: (j, i, 0, 0, 0))
    operands = [proj, proj, proj, proj, gn_g, states]
    in_specs = [col(0), col(1), col(2), col(3), _resident(gn_g), s_spec]
    aliases = {}
    if new_states is not None:
        aliases = {len(operands): 1}
        operands.append(new_states)
        in_specs.append(pl.BlockSpec(memory_space=pl.ANY))
    return pl.pallas_call(
        functools.partial(_ret_sample_kernel, gamma),
        grid=(n_rows // n,),
        in_specs=in_specs,
        out_specs=[pl.BlockSpec((n, RET_WIDTH), lambda i: (i, 0)), s_spec],
        out_shape=[jax.ShapeDtypeStruct((n_rows, RET_WIDTH), BF16),
                   jax.ShapeDtypeStruct(states.shape, F32)],
        input_output_aliases=aliases,
        compiler_params=pltpu.CompilerParams(dimension_semantics=("parallel",),
                                             vmem_limit_bytes=VMEM_LIMIT),
    )(*operands)


def _lru_gates(uc, w_gates, b_gates, lam):
    pre = _dot(uc.astype(BF16), w_gates) + b_gates
    r = _sigmoid(pre[:, :LRU_WIDTH])
    i = _sigmoid(pre[:, LRU_WIDTH:])
    neg = -lam
    softplus = jnp.maximum(neg, 0.0) + jnp.log1p(jnp.exp(-jnp.abs(neg)))
    log_a = (-LRU_C * softplus) * r
    a = jnp.exp(log_a)
    xin = jnp.sqrt(jnp.tanh(-log_a) * (1.0 + a * a)) * (i * uc)
    return a, xin


def _lru_prompt_kernel(ux_ref, ug_ref, cw_ref, cb_ref, wg_ref, bg_ref, lam_ref,
                       o_ref, h_ref, tail_ref, ext, a_scr, x_scr, h_scr, carry):
    t = pl.program_id(1)
    n = SEQ_TILE
    pad = SUBLANES

    @pl.when(t == 0)
    def _():
        ext[0:pad, :] = jnp.zeros((pad, LRU_WIDTH), F32)
        carry[...] = jnp.zeros_like(carry)

    u = ux_ref[...]
    ext[pad:pad + n, :] = u
    uc = cb_ref[...] + cw_ref[CONV_W - 1:CONV_W, :] * u
    for j in range(1, CONV_W):
        uc = uc + cw_ref[CONV_W - 1 - j:CONV_W - j, :] * ext[pad - j:pad - j + n, :]
    ext[0:pad, :] = u[n - pad:, :]
    tail_ref[0] = u[n - pad:, :]

    a, x = _lru_gates(uc, wg_ref[...], bg_ref[...], lam_ref[...])

    groups = (n // SUBLANES, SUBLANES, LRU_WIDTH)
    a = a.reshape(groups)
    x = x.reshape(groups)
    step = lax.broadcasted_iota(jnp.int32, groups, 1)
    shift = 1
    while shift < SUBLANES:
        inside = step >= shift
        x = a * jnp.where(inside, pltpu.roll(x, shift, 1), 0.0) + x
        a = a * jnp.where(inside, pltpu.roll(a, shift, 1), 1.0)
        shift *= 2
    a_scr[...] = a.reshape(n, LRU_WIDTH)
    x_scr[...] = x.reshape(n, LRU_WIDTH)

    def group(i, h):
        r = pl.ds(pl.multiple_of(i * SUBLANES, SUBLANES), SUBLANES)
        hg = a_scr[r, :] * h + x_scr[r, :]
        h_scr[r, :] = hg
        return jnp.broadcast_to(hg[SUBLANES - 1:SUBLANES, :], (SUBLANES, LRU_WIDTH))

    h_last = lax.fori_loop(0, n // SUBLANES, group, carry[...], unroll=8)
    carry[...] = h_last
    h_ref[0] = h_last
    o_ref[...] = (h_scr[...] * _gelu_tanh(ug_ref[...])).astype(o_ref.dtype)


def _lru_prompt(proj, conv_w, conv_b, w_gates, b_gates, lam, batch, seq):
    n_t = seq // SEQ_TILE
    ux_col = 2 * RET_WIDTH * 2 // LRU_WIDTH
    col = lambda j: pl.BlockSpec((SEQ_TILE, LRU_WIDTH), lambda b, t: (b * n_t + t, j))
    small = pl.BlockSpec((1, SUBLANES, LRU_WIDTH), lambda b, t: (b, 0, 0))
    consts = [conv_w, conv_b, w_gates, b_gates, lam]
    return pl.pallas_call(
        _lru_prompt_kernel,
        grid=(batch, n_t),
        in_specs=[col(ux_col), col(ux_col + 1)] + [_resident(c) for c in consts],
        out_specs=[pl.BlockSpec((SEQ_TILE, LRU_WIDTH), lambda b, t: (b * n_t + t, 0)), small, small],
        out_shape=[jax.ShapeDtypeStruct((batch * seq, LRU_WIDTH), BF16),
                   jax.ShapeDtypeStruct((batch, SUBLANES, LRU_WIDTH), F32),
                   jax.ShapeDtypeStruct((batch, SUBLANES, LRU_WIDTH), F32)],
        scratch_shapes=[pltpu.VMEM((SEQ_TILE + SUBLANES, LRU_WIDTH), F32),
                        pltpu.VMEM((SEQ_TILE, LRU_WIDTH), F32),
                        pltpu.VMEM((SEQ_TILE, LRU_WIDTH), F32),
                        pltpu.VMEM((SEQ_TILE, LRU_WIDTH), F32),
                        pltpu.VMEM((SUBLANES, LRU_WIDTH), F32)],
        compiler_params=pltpu.CompilerParams(dimension_semantics=("parallel", "arbitrary"),
                                             vmem_limit_bytes=VMEM_LIMIT),
    )(proj, proj, *consts)


def _lru_sample_kernel(ux_ref, ug_ref, b0_ref, b1_ref, b2_ref, h0_ref, cw_ref, cb_ref, wg_ref, bg_ref,
                       lam_ref, o_ref, h_ref):
    u = ux_ref[...]
    uc = (cb_ref[...] + cw_ref[3:4, :] * u + cw_ref[2:3, :] * b2_ref[...]
          + cw_ref[1:2, :] * b1_ref[...] + cw_ref[0:1, :] * b0_ref[...])
    a, x = _lru_gates(uc, wg_ref[...], bg_ref[...], lam_ref[...])
    h = a * h0_ref[...] + x
    h_ref[...] = h
    o_ref[...] = (h * _gelu_tanh(ug_ref[...])).astype(o_ref.dtype)


def _lru_sample(proj, conv_buf, h0, conv_w, conv_b, w_gates, b_gates, lam):
    n_rows = proj.shape[0]
    ux_col = 2 * RET_WIDTH * 2 // LRU_WIDTH
    col = lambda j: pl.BlockSpec((n_rows, LRU_WIDTH), lambda i: (0, j))
    bufs = [conv_buf[:, j, :] for j in range(CONV_W - 1)]
    rest = bufs + [h0, conv_w, conv_b, w_gates, b_gates, lam]
    return pl.pallas_call(
        _lru_sample_kernel,
        grid=(1,),
        in_specs=[col(ux_col), col(ux_col + 1)] + [_resident(c) for c in rest],
        out_specs=[pl.BlockSpec((n_rows, LRU_WIDTH), lambda i: (0, 0))] * 2,
        out_shape=[jax.ShapeDtypeStruct((n_rows, LRU_WIDTH), BF16),
                   jax.ShapeDtypeStruct((n_rows, LRU_WIDTH), F32)],
        compiler_params=pltpu.CompilerParams(vmem_limit_bytes=VMEM_LIMIT),
    )(proj, proj, *rest)


def _pool_finish(x, deltas, w_ref, pb_ref, ps_ref, g_ref, b_ref):
    ys = [_dot(d.astype(BF16), w_ref[gi]) for gi, d in enumerate(deltas)]
    y = (jnp.concatenate(ys, axis=1) + pb_ref[...]) * ps_ref[...]
    return _layer_norm(DN_ALPHA * x + y, g_ref[...], b_ref[...])


POOL_HALO = POOL_BUF + 1


def _pool_ffn_prompt_body(tiles_per_seq, step, rows, consts, scratch):
    (x,) = rows
    (ext,) = scratch
    n = ROW_TILE
    t = step % tiles_per_seq

    @pl.when(t == 0)
    def _():
        ext[0:POOL_HALO, :] = jnp.zeros((POOL_HALO, D_MODEL), F32)

    ext[POOL_HALO:POOL_HALO + n, :] = x
    pos = t * n + lax.broadcasted_iota(jnp.int32, (n, POOL_GROUP), 0)
    deltas = []
    for gi, wnd in enumerate(POOL_WINDOWS):
        cols = slice(gi * POOL_GROUP, (gi + 1) * POOL_GROUP)
        w = ext[:, cols]
        shift = 1
        while shift < wnd:
            w = w + pltpu.roll(w, shift, 0)
            shift *= 2
        cnt = jnp.minimum(wnd, pos + 1).astype(F32)
        deltas.append(w[POOL_HALO:, :] / cnt - x[:, cols])
    ext[0:POOL_HALO, :] = x[n - POOL_HALO:, :]
    return [_ffn_ln(_pool_finish(x, deltas, *consts[:5]), consts[5:])]


def _ffn_after_pool_sample_body(rows, consts):
    (x,) = rows
    return [_ffn_ln(x, consts[5:])]


def _pool_sample_kernel(x_ref, buf_ref, w_ref, pb_ref, ps_ref, g_ref, b_ref, o_ref):
    x = x_ref[...]
    deltas = []
    for gi, wnd in enumerate(POOL_WINDOWS):
        cols = slice(gi * POOL_GROUP, (gi + 1) * POOL_GROUP)
        s = x[:, cols]
        for j in range(1, wnd):
            s = s + buf_ref[POOL_BUF - j, :, cols]
        cnt = min(float(wnd), PAST_LEN + 1.0)
        deltas.append(s / cnt - x[:, cols])
    o_ref[...] = _pool_finish(x, deltas, w_ref, pb_ref, ps_ref, g_ref, b_ref)


def _pool_sample(x, buf_t, consts):
    full = lambda a: pl.BlockSpec(a.shape, lambda i: (0,) * a.ndim)
    consts = [c if isinstance(c, tuple) else (c, _resident(c)) for c in consts]
    return pl.pallas_call(
        _pool_sample_kernel,
        grid=(1,),
        in_specs=[full(x), full(buf_t)] + [spec for _, spec in consts],
        out_specs=full(x),
        out_shape=jax.ShapeDtypeStruct(x.shape, F32),
        compiler_params=pltpu.CompilerParams(vmem_limit_bytes=VMEM_LIMIT),
    )(x, buf_t, *[a for a, _ in consts])


def _rotary_tables(positions):
    half = HEAD_DIM // 2
    inv = ROPE_BASE ** (-jnp.arange(half, dtype=F32) / half)
    ang = positions.astype(F32)[:, None] * inv[None, :]
    cos, sin = jnp.cos(ang), jnp.sin(ang)
    return jnp.concatenate([cos, cos], axis=1), jnp.concatenate([-sin, sin], axis=1)


def _block_diag(w):
    nb, n, _ = w.shape
    eye = jnp.eye(nb, dtype=w.dtype)
    return (eye[:, None, :, None] * w[:, :, None, :]).reshape(nb * n, nb * n)


def kernel(x_prompt, x_sample, state_ret, state_lru_h, state_lru_conv, state_pool, w_ffn_gate, w_ffn_up, w_ffn_down, ln_g, ln_b, w_mix_in, w_mix_out, ret_gn_g, lru_conv_w, lru_conv_b, lru_wa, lru_ba, lru_wi, lru_bi, lru_lambda, pool_w, pool_b, pool_scale):
    batch, seq, _ = x_prompt.shape
    n_s = x_sample.shape[0]
    assert x_sample.shape[1] == 1 and seq % SEQ_TILE == 0 and n_s % SAMPLE_GROUP == 0
    xp = x_prompt.reshape(batch * seq, D_MODEL)
    xs = x_sample.reshape(n_s, D_MODEL)
    row = lambda a: a.reshape(1, -1)

    cos_p, sin_p = _rotary_tables(jnp.arange(seq))
    cos_s, sin_s = _rotary_tables(jnp.full((n_s,), PAST_LEN))

    wg_all, wu_all, wd_all = (w.astype(BF16) for w in (w_ffn_gate, w_ffn_up, w_ffn_down))
    w_in_all, w_out_all, pool_w_all = (w.astype(BF16) for w in (w_mix_in, w_mix_out, pool_w))

    def ffn_consts(layer, k):
        return [_pick(wg_all, layer, k), _pick(wu_all, layer, k), _pick(wd_all, layer, k),
                row(ln_g[layer, 2 * k]), row(ln_b[layer, 2 * k])]

    out_x = [(D_MODEL, F32)]
    rets_p, hs_p, convs_p, pools_p = [], [], [], []
    hs_s, convs_s, pools_s = [], [], []
    rets_s = None
    for layer in range(DEPTH):
        j = layer // 2
        (xp,), (xs,) = _tokenwise(_same(_ffn_ln_body), _ffn_ln_body, [xp], [xs], ffn_consts(layer, 0), out_x)
        if layer % 2 == 0:
            (proj_p,), (proj_s,) = _tokenwise(
                _same(_mix_in_body), _mix_in_body, [xp, cos_p, sin_p], [xs, cos_s, sin_s],
                [_pick(w_in_all, j)], [(w_mix_in.shape[2], F32)],
                row_maps=[None, seq // ROW_TILE, seq // ROW_TILE])
            gn = row(ret_gn_g[j])
            w_gates = jnp.concatenate([_block_diag(lru_wa[j]), _block_diag(lru_wi[j])], axis=1).astype(BF16)
            b_gates = row(jnp.concatenate([lru_ba[j], lru_bi[j]]))
            lru_consts = (lru_conv_w[j], row(lru_conv_b[j]), w_gates, b_gates, row(lru_lambda[j]))

            ret_p, s_p = _ret_prompt(proj_p, gn, batch, seq)
            lru_p, h_p, tail_p = _lru_prompt(proj_p, *lru_consts, batch, seq)
            ret_s, rets_s = _ret_sample(proj_s, gn, state_ret, j, rets_s)
            lru_s, h_s = _lru_sample(proj_s, state_lru_conv[j], state_lru_h[j], *lru_consts)

            rets_p.append(s_p)
            hs_p.append(h_p[:, 0, :])
            convs_p.append(tail_p[:, SUBLANES - (CONV_W - 1):, :])
            hs_s.append(h_s)
            ux_s = proj_s[:, 4 * RET_WIDTH:4 * RET_WIDTH + LRU_WIDTH]
            convs_s.append(jnp.concatenate([state_lru_conv[j][:, 1:, :], ux_s[:, None, :]], axis=1))

            consts = [_pick(w_out_all, j, rows=(RET_WIDTH, 0)), _pick(w_out_all, j, rows=(LRU_WIDTH, 1)),
                      row(ln_g[layer, 1]), row(ln_b[layer, 1])] + ffn_consts(layer, 1)
            (xp,), (xs,) = _tokenwise(_same(_mix_out_ffn_body), _mix_out_ffn_body, [xp, ret_p, lru_p],
                                      [xs, ret_s, lru_s], consts, out_x)
        else:
            pool_consts = [_pick(pool_w_all, j), row(pool_b[j]), row(pool_scale[j]),
                           row(ln_g[layer, 1]), row(ln_b[layer, 1])]
            pools_p.append(xp.reshape(batch, seq, D_MODEL)[:, seq - POOL_BUF:, :])
            pools_s.append(jnp.concatenate([state_pool[j][:, 1:, :], xs[:, None, :]], axis=1))
            xs = _pool_sample(xs, jnp.transpose(state_pool[j], (1, 0, 2)), pool_consts)
            (xp,), (xs,) = _tokenwise(
                functools.partial(_pool_ffn_prompt_body, seq // ROW_TILE), _ffn_after_pool_sample_body,
                [xp], [xs], pool_consts + ffn_consts(layer, 1), out_x,
                scratch=[pltpu.VMEM((ROW_TILE + POOL_HALO, D_MODEL), F32)])

    return (xp.reshape(batch, seq, D_MODEL), xs.reshape(n_s, 1, D_MODEL),
            jnp.stack(rets_p), jnp.stack(hs_p), jnp.stack(convs_p), jnp.stack(pools_p),
            rets_s, jnp.stack(hs_s), jnp.stack(convs_s), jnp.stack(pools_s))
```

```python
import functools
import math

import numpy as np
import jax
import jax.numpy as jnp
from jax import lax
from jax.experimental import pallas as pl
from jax.experimental.pallas import tpu as pltpu

F32 = jnp.float32
BF16 = jnp.bfloat16

D_MODEL = 1024
DEPTH = 4
PAST_LEN = 16384
RET_HEADS = 4
HEAD_DIM = 128
RET_WIDTH = RET_HEADS * HEAD_DIM
RET_CHUNK = 128
ROPE_BASE = 10000.0
LRU_WIDTH = 512
LRU_C = 8.0
CONV_W = 4
POOL_WINDOWS = (2, 4, 8, 16)
POOL_GROUP = D_MODEL // len(POOL_WINDOWS)
POOL_BUF = max(POOL_WINDOWS) - 1
LN_EPS = 1e-5
DN_ALPHA = (2.0 * DEPTH) ** 0.25

SUBLANES = 8
BF16_SUBLANES = 16
ROW_TILE = 512
SEQ_TILE = 512
SAMPLE_GROUP = 16
VMEM_LIMIT = 56 * 1024 * 1024


def _sigmoid(x):
    return 0.5 * jnp.tanh(0.5 * x) + 0.5


def _silu(x):
    return x * _sigmoid(x)


def _gelu_tanh(x):
    return x * (0.5 * (1.0 + jnp.tanh(math.sqrt(2.0 / math.pi) * (x + 0.044715 * (x * x * x)))))


def _layer_norm(z, g, b):
    mu = jnp.mean(z, axis=-1, keepdims=True)
    d = z - mu
    var = jnp.mean(d * d, axis=-1, keepdims=True)
    return d * lax.rsqrt(var + LN_EPS) * g + b


def _unit_norm(z):
    mu = jnp.mean(z, axis=-1, keepdims=True)
    d = z - mu
    var = jnp.mean(d * d, axis=-1, keepdims=True)
    return d * lax.rsqrt(var + LN_EPS)


def _dot(a, b):
    return jnp.dot(a, b, preferred_element_type=F32)


def _resident(arr):
    nd = arr.ndim
    return pl.BlockSpec(arr.shape, lambda *_: (0,) * nd, pipeline_mode=pl.Buffered(1))


def _pick(arr, *lead, rows=None):
    tail = arr.shape[len(lead):]
    first = 0
    if rows is not None:
        tail = (rows[0],) + tail[1:]
        first = rows[1]
    index = tuple(lead) + (first,) + (0,) * (len(tail) - 1)
    return arr, pl.BlockSpec((None,) * len(lead) + tail, lambda *_: index, pipeline_mode=pl.Buffered(1))


def _tokenwise(body_p, body_s, rows_p, rows_s, consts, outs, row_maps=None, scratch=()):
    n_rows_p = max(a.shape[0] for a in rows_p)
    n_rows_s = rows_s[0].shape[0]
    assert n_rows_p % ROW_TILE == 0
    n_p = n_rows_p // ROW_TILE
    n_in_p, n_in_s, n_c, n_o = len(rows_p), len(rows_s), len(consts), len(outs)
    row_maps = row_maps or [None] * n_in_p
    consts = [c if isinstance(c, tuple) else (c, _resident(c)) for c in consts]

    def kern(*refs):
        refs = list(refs)
        p_refs, refs = refs[:n_in_p], refs[n_in_p:]
        s_refs, refs = refs[:n_in_s], refs[n_in_s:]
        c_refs, refs = refs[:n_c], refs[n_c:]
        op_refs, refs = refs[:n_o], refs[n_o:]
        os_refs, scr = refs[:n_o], refs[n_o:]
        i = pl.program_id(0)

        @pl.when(i < n_p)
        def _():
            for o, v in zip(op_refs, body_p(i, [r[...] for r in p_refs], c_refs, scr)):
                o[...] = v.astype(o.dtype)

        @pl.when(i == n_p)
        def _():
            for o, v in zip(os_refs, body_s([r[...] for r in s_refs], c_refs)):
                o[...] = v.astype(o.dtype)

    def p_spec(cols, wrap):
        if wrap is None:
            return pl.BlockSpec((ROW_TILE, cols), lambda i: (jnp.minimum(i, n_p - 1), 0))
        return pl.BlockSpec((ROW_TILE, cols), lambda i: (jnp.minimum(i, n_p - 1) % wrap, 0))

    in_specs = ([p_spec(a.shape[1], w) for a, w in zip(rows_p, row_maps)]
                + [pl.BlockSpec(a.shape, lambda i: (0, 0)) for a in rows_s]
                + [spec for _, spec in consts])
    out_specs = ([p_spec(c, None) for c, _ in outs]
                 + [pl.BlockSpec((n_rows_s, c), lambda i: (0, 0)) for c, _ in outs])
    out_shape = ([jax.ShapeDtypeStruct((n_rows_p, c), dt) for c, dt in outs]
                 + [jax.ShapeDtypeStruct((n_rows_s, c), dt) for c, dt in outs])
    res = pl.pallas_call(
        kern, grid=(n_p + 1,), in_specs=in_specs, out_specs=out_specs, out_shape=out_shape,
        scratch_shapes=list(scratch),
        compiler_params=pltpu.CompilerParams(dimension_semantics=("arbitrary",),
                                             vmem_limit_bytes=VMEM_LIMIT),
    )(*rows_p, *rows_s, *[a for a, _ in consts])
    return res[:n_o], res[n_o:]


def _same(body):
    return lambda step, rows, consts, scratch: body(rows, consts)


def _ffn_residual(x, wg, wu, wd, after_gate=None, after_act=None):
    xb = x.astype(BF16)
    hg = _dot(xb, wg[...])
    if after_gate is not None:
        after_gate()
    hu = _dot(xb, wu[...])
    act = (_silu(hg) * hu).astype(BF16)
    if after_act is not None:
        after_act()
    return DN_ALPHA * x + 0.5 * _dot(act, wd[...])


def _ffn_pipeline(pre_p, pre_s, rows_p, rows_s, pre_consts, ffn_consts, scratch=(), cast_next=None):
    n_rows_p = rows_p[0].shape[0]
    n_rows_s = rows_s[0].shape[0]
    assert n_rows_p % ROW_TILE == 0
    n_p = n_rows_p // ROW_TILE
    n_in_p, n_in_s, n_pc = len(rows_p), len(rows_s), len(pre_consts)
    consts = [c if isinstance(c, tuple) else (c, _resident(c)) for c in list(pre_consts) + list(ffn_consts)]
    n_cast = 0 if cast_next is None else 3

    def kern(*refs):
        refs = list(refs)
        p_refs, refs = refs[:n_in_p], refs[n_in_p:]
        s_refs, refs = refs[:n_in_s], refs[n_in_s:]
        pc_refs, refs = refs[:n_pc], refs[n_pc:]
        (wg, wu, wd, ln_g, ln_b), refs = refs[:5], refs[5:]
        cast_in, refs = refs[:n_cast], refs[n_cast:]
        (out_p, out_s), refs = refs[:2], refs[2:]
        cast_out, refs = refs[:n_cast], refs[n_cast:]
        (x2_even, x2_odd, z_scr), scr = refs[:3], refs[3:]
        g = pl.program_id(0)
        steady = (g >= 1) & (g <= n_p)


        @pl.when(g == 0)
        def _():
            z_scr[...] = jnp.zeros_like(z_scr)
            x2_even[...] = pre_p(g, [r[...] for r in p_refs], pc_refs, scr)

        def steady_step(x2_cur, x2_next):
            def pre_next():
                x2_next[...] = pre_p(g, [r[...] for r in p_refs], pc_refs, scr)

            def norm_prev():
                out_p[...] = _layer_norm(z_scr[...], ln_g[...], ln_b[...])

            def others():
                norm_prev()
                pre_next()
                for src, dst in zip(cast_in, cast_out):
                    dst[...] = src[...].astype(dst.dtype)

            z_scr[...] = _ffn_residual(x2_cur[...], wg, wu, wd, None, others)

        pl.when(steady & (g % 2 == 1))(lambda: steady_step(x2_even, x2_odd))
        pl.when(steady & (g % 2 == 0))(lambda: steady_step(x2_odd, x2_even))

        @pl.when(g == n_p + 1)
        def _():
            out_p[...] = _layer_norm(z_scr[...], ln_g[...], ln_b[...])
            x2 = pre_s([r[...] for r in s_refs], pc_refs)
            out_s[...] = _layer_norm(_ffn_residual(x2, wg, wu, wd), ln_g[...], ln_b[...])

    tile = lambda cols, lag: pl.BlockSpec(
        (ROW_TILE, cols), lambda g: (jnp.clip(g - lag, 0, n_p - 1), 0))
    in_specs = ([tile(a.shape[1], 0) for a in rows_p]
                + [pl.BlockSpec(a.shape, lambda g: (0, 0)) for a in rows_s]
                + [spec for _, spec in consts])
    out_specs = [tile(D_MODEL, 2), pl.BlockSpec((n_rows_s, D_MODEL), lambda g: (0, 0))]
    out_shape = [jax.ShapeDtypeStruct((n_rows_p, D_MODEL), F32), jax.ShapeDtypeStruct((n_rows_s, D_MODEL), F32)]
    cast_operands = []
    if cast_next is not None:
        *stacked, layer, k = cast_next
        for w in stacked:
            rows, cols = w.shape[2:]
            size = next(s for s in range(BF16_SUBLANES, rows + 1, BF16_SUBLANES)
                        if rows % s == 0 and rows // s <= n_p)
            block = lambda g, last=rows // size - 1: jnp.clip(g - 1, 0, last)
            in_specs.append(pl.BlockSpec((None, None, size, cols), lambda g, b=block: (layer, k, b(g), 0)))
            out_specs.append(pl.BlockSpec((size, cols), lambda g, b=block: (b(g), 0)))
            out_shape.append(jax.ShapeDtypeStruct((rows, cols), BF16))
            cast_operands.append(w)
    return pl.pallas_call(
        kern, grid=(n_p + 2,), in_specs=in_specs, out_specs=out_specs, out_shape=out_shape,
        scratch_shapes=[pltpu.VMEM((ROW_TILE, D_MODEL), F32)] * 3 + list(scratch),
        compiler_params=pltpu.CompilerParams(dimension_semantics=("arbitrary",),
                                             vmem_limit_bytes=VMEM_LIMIT),
    )(*rows_p, *rows_s, *[a for a, _ in consts], *cast_operands)


def _identity_pre(rows, consts):
    (x,) = rows
    return x


def _mix_in_body(rows, consts):
    x, cos, sin = rows
    (w,) = consts
    proj = _dot(x.astype(BF16), w[...])
    pieces = []
    for j in range(2 * RET_HEADS):
        blk = proj[:, j * HEAD_DIM:(j + 1) * HEAD_DIM]
        rot = blk * cos + pltpu.roll(blk, HEAD_DIM // 2, 1) * sin
        if j < RET_HEADS:
            rot = rot * (HEAD_DIM ** -0.5)
        pieces.append(rot)
    pieces.append(proj[:, 2 * RET_WIDTH:])
    return [jnp.concatenate(pieces, axis=1)]


def _mix_out_ln(rows, consts):
    x, ret, lru = rows
    w_ret, w_lru, g, b = consts
    y = _dot(ret, w_ret[...]) + _dot(lru, w_lru[...])
    return _layer_norm(DN_ALPHA * x + y, g[...], b[...])


def _retention_tables():
    lg = np.log1p(-np.exp2(-5.0 - np.arange(RET_HEADS, dtype=np.float64)))
    idx = np.arange(RET_CHUNK, dtype=np.float64)
    diff = idx[:, None] - idx[None, :]
    decay = np.where(diff >= 0, np.exp(lg[:, None, None] * np.maximum(diff, 0.0)), 0.0)
    ones = np.ones((1, 1, HEAD_DIM))
    q_dec = np.exp(lg[:, None] * (idx + 1.0))[..., None] * ones
    k_dec = np.exp(lg[:, None] * (RET_CHUNK - 1.0 - idx))[..., None] * ones
    c_dec = np.exp(lg * RET_CHUNK)
    as32 = lambda a: jnp.asarray(a.astype(np.float32))
    return as32(decay), as32(q_dec), as32(k_dec), [float(c) for c in c_dec], [float(g) for g in np.exp(lg)]


def _ret_prompt_tile(c_dec, q_ref, k_ref, v_ref, g_ref, gn_ref, dec_ref, qd_ref, kd_ref, o_ref, s_ref):
    states = [s_ref[0, h] for h in range(RET_HEADS)]
    for c in range(SEQ_TILE // RET_CHUNK):
        rows = slice(c * RET_CHUNK, (c + 1) * RET_CHUNK)
        for h in range(RET_HEADS):
            cols = slice(h * HEAD_DIM, (h + 1) * HEAD_DIM)
            s = states[h]
            qc = q_ref[rows, cols]
            kc = k_ref[rows, cols]
            vb = v_ref[rows, cols].astype(BF16)
            scores = lax.dot_general(qc.astype(BF16), kc.astype(BF16), (((1,), (1,)), ((), ())),
                                     preferred_element_type=F32) * dec_ref[h]
            lhs = jnp.concatenate([scores.astype(BF16), (qc * qd_ref[h]).astype(BF16)], axis=1)
            o = _dot(lhs, jnp.concatenate([vb, s.astype(BF16)], axis=0))
            states[h] = s * c_dec[h] + _dot((kc * kd_ref[h]).T.astype(BF16), vb)
            o = _unit_norm(o) * gn_ref[:, cols]
            o_ref[rows, cols] = (_silu(g_ref[rows, cols]) * o).astype(o_ref.dtype)
    for h in range(RET_HEADS):
        s_ref[0, h] = states[h]


def _ret_sample_kernel(gamma, q_ref, k_ref, v_ref, g_ref, gn_ref, s_in, *rest):
    o_ref, s_out = rest[-2:]
    n = SAMPLE_GROUP
    row = lax.broadcasted_iota(jnp.int32, (n, n * HEAD_DIM), 0)
    blk = lax.broadcasted_iota(jnp.int32, (n, n * HEAD_DIM), 1) // HEAD_DIM
    own = row == blk
    for h in range(RET_HEADS):
        cols = slice(h * HEAD_DIM, (h + 1) * HEAD_DIM)
        q = q_ref[:, cols]
        k = k_ref[:, cols]
        v = v_ref[:, cols]
        v_wide = jnp.where(own, jnp.concatenate([v] * n, axis=1), 0.0).astype(BF16)
        outer = _dot(k.T.astype(BF16), v_wide)
        new = []
        for r in range(n):
            s_new = s_in[r, h] * gamma[h] + outer[:, r * HEAD_DIM:(r + 1) * HEAD_DIM]
            s_out[r, h] = s_new
            new.append(s_new.astype(BF16))
        wide = _dot(q.astype(BF16), jnp.concatenate(new, axis=1))
        wide = jnp.where(own, wide, 0.0)
        o = wide[:, :HEAD_DIM]
        for r in range(1, n):
            o = o + wide[:, r * HEAD_DIM:(r + 1) * HEAD_DIM]
        o = _unit_norm(o) * gn_ref[:, cols]
        o_ref[:, cols] = (_silu(g_ref[:, cols]) * o).astype(o_ref.dtype)


def _ret_sample(proj, gn_g, states, j, new_states):
    _, _, _, _, gamma = _retention_tables()
    n_rows = proj.shape[0]
    n = SAMPLE_GROUP
    col = lambda c: pl.BlockSpec((n, RET_WIDTH), lambda i: (i, c))
    s_spec = pl.BlockSpec((None, n, RET_HEADS, HEAD_DIM, HEAD_DIM), lambda i: (j, i, 0, 0, 0))
    operands = [proj, proj, proj, proj, gn_g, states]
    in_specs = [col(0), col(1), col(2), col(3), _resident(gn_g), s_spec]
    aliases = {}
    if new_states is not None:
        aliases = {len(operands): 1}
        operands.append(new_states)
        in_specs.append(pl.BlockSpec(memory_space=pl.ANY))
    return pl.pallas_call(
        functools.partial(_ret_sample_kernel, gamma),
        grid=(n_rows // n,),
        in_specs=in_specs,
        out_specs=[pl.BlockSpec((n, RET_WIDTH), lambda i: (i, 0)), s_spec],
        out_shape=[jax.ShapeDtypeStruct((n_rows, RET_WIDTH), BF16),
                   jax.ShapeDtypeStruct(states.shape, F32)],
        input_output_aliases=aliases,
        compiler_params=pltpu.CompilerParams(dimension_semantics=("parallel",),
                                             vmem_limit_bytes=VMEM_LIMIT),
    )(*operands)


def _lru_gates(uc, w_gates, b_gates, lam):
    pre = _dot(uc.astype(BF16), w_gates) + b_gates
    r = _sigmoid(pre[:, :LRU_WIDTH])
    i = _sigmoid(pre[:, LRU_WIDTH:])
    neg = -lam
    softplus = jnp.maximum(neg, 0.0) + jnp.log1p(jnp.exp(-jnp.abs(neg)))
    log_a = (-LRU_C * softplus) * r
    a = jnp.exp(log_a)
    xin = jnp.sqrt(jnp.tanh(-log_a) * (1.0 + a * a)) * (i * uc)
    return a, xin


def _lru_prompt_tile(ux_ref, ug_ref, cw_ref, cb_ref, wg_ref, bg_ref, lam_ref,
                     o_ref, h_ref, tail_ref, ext, a_scr, x_scr, h_scr):
    n = SEQ_TILE
    pad = SUBLANES
    u = ux_ref[...]
    ext[pad:pad + n, :] = u
    uc = cb_ref[...] + cw_ref[CONV_W - 1:CONV_W, :] * u
    for j in range(1, CONV_W):
        uc = uc + cw_ref[CONV_W - 1 - j:CONV_W - j, :] * ext[pad - j:pad - j + n, :]
    ext[0:pad, :] = u[n - pad:, :]
    tail_ref[0] = u[n - pad:, :]

    a, x = _lru_gates(uc, wg_ref[...], bg_ref[...], lam_ref[...])

    groups = (n // SUBLANES, SUBLANES, LRU_WIDTH)
    a = a.reshape(groups)
    x = x.reshape(groups)
    step = lax.broadcasted_iota(jnp.int32, groups, 1)
    shift = 1
    while shift < SUBLANES:
        inside = step >= shift
        x = a * jnp.where(inside, pltpu.roll(x, shift, 1), 0.0) + x
        a = a * jnp.where(inside, pltpu.roll(a, shift, 1), 1.0)
        shift *= 2
    a_scr[...] = a.reshape(n, LRU_WIDTH)
    x_scr[...] = x.reshape(n, LRU_WIDTH)

    def group(i, h):
        r = pl.ds(pl.multiple_of(i * SUBLANES, SUBLANES), SUBLANES)
        hg = a_scr[r, :] * h + x_scr[r, :]
        h_scr[r, :] = hg
        return jnp.broadcast_to(hg[SUBLANES - 1:SUBLANES, :], (SUBLANES, LRU_WIDTH))

    h_ref[0] = lax.fori_loop(0, n // SUBLANES, group, h_ref[0], unroll=True)
    o_ref[...] = (h_scr[...] * _gelu_tanh(ug_ref[...])).astype(o_ref.dtype)


def _mixer_prompt_kernel(c_dec, q_ref, k_ref, v_ref, g_ref, ux_ref, ug_ref, gn_ref, dec_ref, qd_ref, kd_ref,
                         cw_ref, cb_ref, wg_ref, bg_ref, lam_ref,
                         ret_ref, s_ref, lru_ref, h_ref, tail_ref, ext, a_scr, x_scr, h_scr):
    @pl.when(pl.program_id(1) == 0)
    def _():
        s_ref[...] = jnp.zeros_like(s_ref)
        h_ref[...] = jnp.zeros_like(h_ref)
        ext[0:SUBLANES, :] = jnp.zeros((SUBLANES, LRU_WIDTH), F32)

    _lru_prompt_tile(ux_ref, ug_ref, cw_ref, cb_ref, wg_ref, bg_ref, lam_ref, lru_ref, h_ref, tail_ref,
                     ext, a_scr, x_scr, h_scr)
    _ret_prompt_tile(c_dec, q_ref, k_ref, v_ref, g_ref, gn_ref, dec_ref, qd_ref, kd_ref, ret_ref, s_ref)


def _mixer_prompt(proj, gn_g, conv_w, conv_b, w_gates, b_gates, lam, batch, seq):
    decay, q_dec, k_dec, c_dec, _ = _retention_tables()
    n_t = seq // SEQ_TILE
    col = lambda j: pl.BlockSpec((SEQ_TILE, RET_WIDTH), lambda b, t: (b * n_t + t, j))
    out_col = pl.BlockSpec((SEQ_TILE, RET_WIDTH), lambda b, t: (b * n_t + t, 0))
    small = pl.BlockSpec((1, SUBLANES, LRU_WIDTH), lambda b, t: (b, 0, 0))
    consts = [gn_g, decay, q_dec, k_dec, conv_w, conv_b, w_gates, b_gates, lam]
    tile_f32 = pltpu.VMEM((SEQ_TILE, LRU_WIDTH), F32)
    return pl.pallas_call(
        functools.partial(_mixer_prompt_kernel, c_dec),
        grid=(batch, n_t),
        in_specs=[col(j) for j in range(6)] + [_resident(c) for c in consts],
        out_specs=[out_col, pl.BlockSpec((1, RET_HEADS, HEAD_DIM, HEAD_DIM), lambda b, t: (b, 0, 0, 0)),
                   out_col, small, small],
        out_shape=[jax.ShapeDtypeStruct((batch * seq, RET_WIDTH), BF16),
                   jax.ShapeDtypeStruct((batch, RET_HEADS, HEAD_DIM, HEAD_DIM), F32),
                   jax.ShapeDtypeStruct((batch * seq, LRU_WIDTH), BF16),
                   jax.ShapeDtypeStruct((batch, SUBLANES, LRU_WIDTH), F32),
                   jax.ShapeDtypeStruct((batch, SUBLANES, LRU_WIDTH), F32)],
        scratch_shapes=[pltpu.VMEM((SEQ_TILE + SUBLANES, LRU_WIDTH), F32), tile_f32, tile_f32, tile_f32],
        compiler_params=pltpu.CompilerParams(dimension_semantics=("parallel", "arbitrary"),
                                             vmem_limit_bytes=VMEM_LIMIT),
    )(*([proj] * 6), *consts)


def _lru_sample_kernel(ux_ref, ug_ref, b0_ref, b1_ref, b2_ref, h0_ref, cw_ref, cb_ref, wg_ref, bg_ref,
                       lam_ref, o_ref, h_ref):
    u = ux_ref[...]
    uc = (cb_ref[...] + cw_ref[3:4, :] * u + cw_ref[2:3, :] * b2_ref[...]
          + cw_ref[1:2, :] * b1_ref[...] + cw_ref[0:1, :] * b0_ref[...])
    a, x = _lru_gates(uc, wg_ref[...], bg_ref[...], lam_ref[...])
    h = a * h0_ref[...] + x
    h_ref[...] = h
    o_ref[...] = (h * _gelu_tanh(ug_ref[...])).astype(o_ref.dtype)


def _lru_sample(proj, conv_buf, h0, conv_w, conv_b, w_gates, b_gates, lam):
    n_rows = proj.shape[0]
    ux_col = 2 * RET_WIDTH * 2 // LRU_WIDTH
    col = lambda j: pl.BlockSpec((n_rows, LRU_WIDTH), lambda i: (0, j))
    bufs = [conv_buf[:, j, :] for j in range(CONV_W - 1)]
    rest = bufs + [h0, conv_w, conv_b, w_gates, b_gates, lam]
    return pl.pallas_call(
        _lru_sample_kernel,
        grid=(1,),
        in_specs=[col(ux_col), col(ux_col + 1)] + [_resident(c) for c in rest],
        out_specs=[pl.BlockSpec((n_rows, LRU_WIDTH), lambda i: (0, 0))] * 2,
        out_shape=[jax.ShapeDtypeStruct((n_rows, LRU_WIDTH), BF16),
                   jax.ShapeDtypeStruct((n_rows, LRU_WIDTH), F32)],
        compiler_params=pltpu.CompilerParams(vmem_limit_bytes=VMEM_LIMIT),
    )(proj, proj, *rest)


def _pool_finish(x, deltas, w_ref, pb_ref, ps_ref, g_ref, b_ref):
    ys = [_dot(d.astype(BF16), w_ref[gi]) for gi, d in enumerate(deltas)]
    y = (jnp.concatenate(ys, axis=1) + pb_ref[...]) * ps_ref[...]
    return _layer_norm(DN_ALPHA * x + y, g_ref[...], b_ref[...])


POOL_HALO = POOL_BUF + 1


def _pool_ln_prompt(tiles_per_seq, step, rows, consts, scratch):
    (x,) = rows
    (ext,) = scratch
    n = ROW_TILE
    t = step % tiles_per_seq

    @pl.when(t == 0)
    def _():
        ext[0:POOL_HALO, :] = jnp.zeros((POOL_HALO, D_MODEL), F32)

    ext[POOL_HALO:POOL_HALO + n, :] = x
    pos = t * n + lax.broadcasted_iota(jnp.int32, (n, POOL_GROUP), 0)
    deltas = []
    for gi, wnd in enumerate(POOL_WINDOWS):
        cols = slice(gi * POOL_GROUP, (gi + 1) * POOL_GROUP)
        w = ext[:, cols]
        shift = 1
        while shift < wnd:
            w = w + pltpu.roll(w, shift, 0)
            shift *= 2
        cnt = jnp.minimum(wnd, pos + 1).astype(F32)
        deltas.append(w[POOL_HALO:, :] / cnt - x[:, cols])
    ext[0:POOL_HALO, :] = x[n - POOL_HALO:, :]
    return _pool_finish(x, deltas, *consts)


def _pool_sample_kernel(x_ref, buf_ref, w_ref, pb_ref, ps_ref, g_ref, b_ref, o_ref):
    x = x_ref[...]
    deltas = []
    for gi, wnd in enumerate(POOL_WINDOWS):
        cols = slice(gi * POOL_GROUP, (gi + 1) * POOL_GROUP)
        s = x[:, cols]
        for j in range(1, wnd):
            s = s + buf_ref[POOL_BUF - j, :, cols]
        cnt = min(float(wnd), PAST_LEN + 1.0)
        deltas.append(s / cnt - x[:, cols])
    o_ref[...] = _pool_finish(x, deltas, w_ref, pb_ref, ps_ref, g_ref, b_ref)


def _pool_sample(x, buf_t, consts):
    full = lambda a: pl.BlockSpec(a.shape, lambda i: (0,) * a.ndim)
    consts = [c if isinstance(c, tuple) else (c, _resident(c)) for c in consts]
    return pl.pallas_call(
        _pool_sample_kernel,
        grid=(1,),
        in_specs=[full(x), full(buf_t)] + [spec for _, spec in consts],
        out_specs=full(x),
        out_shape=jax.ShapeDtypeStruct(x.shape, F32),
        compiler_params=pltpu.CompilerParams(vmem_limit_bytes=VMEM_LIMIT),
    )(x, buf_t, *[a for a, _ in consts])


def _rotary_tables(positions):
    half = HEAD_DIM // 2
    inv = ROPE_BASE ** (-jnp.arange(half, dtype=F32) / half)
    ang = positions.astype(F32)[:, None] * inv[None, :]
    cos, sin = jnp.cos(ang), jnp.sin(ang)
    return jnp.concatenate([cos, cos], axis=1), jnp.concatenate([-sin, sin], axis=1)


def _block_diag(w):
    nb, n, _ = w.shape
    eye = jnp.eye(nb, dtype=w.dtype)
    return (eye[:, None, :, None] * w[:, :, None, :]).reshape(nb * n, nb * n)


def kernel(x_prompt, x_sample, state_ret, state_lru_h, state_lru_conv, state_pool, w_ffn_gate, w_ffn_up, w_ffn_down, ln_g, ln_b, w_mix_in, w_mix_out, ret_gn_g, lru_conv_w, lru_conv_b, lru_wa, lru_ba, lru_wi, lru_bi, lru_lambda, pool_w, pool_b, pool_scale):
    batch, seq, _ = x_prompt.shape
    n_s = x_sample.shape[0]
    assert x_sample.shape[1] == 1 and seq % SEQ_TILE == 0 and n_s % SAMPLE_GROUP == 0
    xp = x_prompt.reshape(batch * seq, D_MODEL)
    xs = x_sample.reshape(n_s, D_MODEL)
    row = lambda a: a.reshape(1, -1)

    cos_p, sin_p = _rotary_tables(jnp.arange(seq))
    cos_s, sin_s = _rotary_tables(jnp.full((n_s,), PAST_LEN))

    w_in_all, w_out_all, pool_w_all = (w.astype(BF16) for w in (w_mix_in, w_mix_out, pool_w))
    ffn_stacks = (w_ffn_gate, w_ffn_up, w_ffn_down)
    ffn_w = [w[0, 0].astype(BF16) for w in ffn_stacks]

    def ffn(pre_p, pre_s, rows_p, rows_s, pre_consts, layer, k, **kw):
        nonlocal ffn_w
        consts = ffn_w + [row(ln_g[layer, 2 * k]), row(ln_b[layer, 2 * k])]
        nxt = (layer, 1) if k == 0 else (layer + 1, 0)
        cast_next = None if nxt[0] == DEPTH else (*ffn_stacks, *nxt)
        xp, xs, *cast = _ffn_pipeline(pre_p, pre_s, rows_p, rows_s, pre_consts, consts,
                                      cast_next=cast_next, **kw)
        ffn_w = cast
        return xp, xs

    rets_p, hs_p, convs_p, pools_p = [], [], [], []
    hs_s, convs_s, pools_s = [], [], []
    rets_s = None
    for layer in range(DEPTH):
        j = layer // 2
        xp, xs = ffn(_same(_identity_pre), _identity_pre, [xp], [xs], [], layer, 0)
        if layer % 2 == 0:
            (proj_p,), (proj_s,) = _tokenwise(
                _same(_mix_in_body), _mix_in_body, [xp, cos_p, sin_p], [xs, cos_s, sin_s],
                [_pick(w_in_all, j)], [(w_mix_in.shape[2], F32)],
                row_maps=[None, seq // ROW_TILE, seq // ROW_TILE])
            gn = row(ret_gn_g[j])
            w_gates = jnp.concatenate([_block_diag(lru_wa[j]), _block_diag(lru_wi[j])], axis=1).astype(BF16)
            b_gates = row(jnp.concatenate([lru_ba[j], lru_bi[j]]))
            lru_consts = (lru_conv_w[j], row(lru_conv_b[j]), w_gates, b_gates, row(lru_lambda[j]))

            ret_p, s_p, lru_p, h_p, tail_p = _mixer_prompt(proj_p, gn, *lru_consts, batch, seq)
            ret_s, rets_s = _ret_sample(proj_s, gn, state_ret, j, rets_s)
            lru_s, h_s = _lru_sample(proj_s, state_lru_conv[j], state_lru_h[j], *lru_consts)

            rets_p.append(s_p)
            hs_p.append(h_p[:, 0, :])
            convs_p.append(tail_p[:, SUBLANES - (CONV_W - 1):, :])
            hs_s.append(h_s)
            ux_s = proj_s[:, 4 * RET_WIDTH:4 * RET_WIDTH + LRU_WIDTH]
            convs_s.append(jnp.concatenate([state_lru_conv[j][:, 1:, :], ux_s[:, None, :]], axis=1))

            consts = [_pick(w_out_all, j, rows=(RET_WIDTH, 0)), _pick(w_out_all, j, rows=(LRU_WIDTH, 1)),
                      row(ln_g[layer, 1]), row(ln_b[layer, 1])]
            xp, xs = ffn(_same(_mix_out_ln), _mix_out_ln, [xp, ret_p, lru_p], [xs, ret_s, lru_s], consts, layer, 1)
        else:
            pool_consts = [_pick(pool_w_all, j), row(pool_b[j]), row(pool_scale[j]),
                           row(ln_g[layer, 1]), row(ln_b[layer, 1])]
            pools_p.append(xp.reshape(batch, seq, D_MODEL)[:, seq - POOL_BUF:, :])
            pools_s.append(jnp.concatenate([state_pool[j][:, 1:, :], xs[:, None, :]], axis=1))
            xs = _pool_sample(xs, jnp.transpose(state_pool[j], (1, 0, 2)), pool_consts)
            xp, xs = ffn(functools.partial(_pool_ln_prompt, seq // ROW_TILE), lambda rows, consts: rows[0],
                         [xp], [xs], pool_consts, layer, 1,
                         scratch=[pltpu.VMEM((ROW_TILE + POOL_HALO, D_MODEL), F32)])

    return (xp.reshape(batch, seq, D_MODEL), xs.reshape(n_s, 1, D_MODEL),
            jnp.stack(rets_p), jnp.stack(hs_p), jnp.stack(convs_p), jnp.stack(pools_p),
            rets_s, jnp.stack(hs_s), jnp.stack(convs_s), jnp.stack(pools_s))
```

```python
import functools
import math

import numpy as np
import jax
import jax.numpy as jnp
from jax import lax
from jax.experimental import pallas as pl
from jax.experimental.pallas import tpu as pltpu

F32 = jnp.float32
BF16 = jnp.bfloat16

D_MODEL = 1024
DEPTH = 4
PAST_LEN = 16384
RET_HEADS = 4
HEAD_DIM = 128
RET_WIDTH = RET_HEADS * HEAD_DIM
RET_CHUNK = 128
ROPE_BASE = 10000.0
LRU_WIDTH = 512
LRU_C = 8.0
CONV_W = 4
POOL_WINDOWS = (2, 4, 8, 16)
POOL_GROUP = D_MODEL // len(POOL_WINDOWS)
POOL_BUF = max(POOL_WINDOWS) - 1
LN_EPS = 1e-5
DN_ALPHA = (2.0 * DEPTH) ** 0.25

SUBLANES = 8
BF16_SUBLANES = 16
ROW_TILE = 512
SEQ_TILE = 512
SAMPLE_GROUP = 16
VMEM_LIMIT = 56 * 1024 * 1024


def _sigmoid(x):
    return 0.5 * jnp.tanh(0.5 * x) + 0.5


def _silu(x):
    return x * _sigmoid(x)


def _gelu_tanh(x):
    return x * (0.5 * (1.0 + jnp.tanh(math.sqrt(2.0 / math.pi) * (x + 0.044715 * (x * x * x)))))


def _layer_norm(z, g, b):
    mu = jnp.mean(z, axis=-1, keepdims=True)
    d = z - mu
    var = jnp.mean(d * d, axis=-1, keepdims=True)
    return d * lax.rsqrt(var + LN_EPS) * g + b


def _unit_norm(z):
    mu = jnp.mean(z, axis=-1, keepdims=True)
    d = z - mu
    var = jnp.mean(d * d, axis=-1, keepdims=True)
    return d * lax.rsqrt(var + LN_EPS)


def _dot(a, b):
    return jnp.dot(a, b, preferred_element_type=F32)


def _resident(arr):
    nd = arr.ndim
    return pl.BlockSpec(arr.shape, lambda *_: (0,) * nd, pipeline_mode=pl.Buffered(1))


def _pick(arr, *lead, rows=None):
    tail = arr.shape[len(lead):]
    first = 0
    if rows is not None:
        tail = (rows[0],) + tail[1:]
        first = rows[1]
    index = tuple(lead) + (first,) + (0,) * (len(tail) - 1)
    return arr, pl.BlockSpec((None,) * len(lead) + tail, lambda *_: index, pipeline_mode=pl.Buffered(1))


def _same(body):
    return lambda step, rows, consts, scratch: body(rows, consts)


def _ffn_residual(x, wg, wu, wd, after_gate=None, after_act=None):
    xb = x.astype(BF16)
    hg = _dot(xb, wg[...])
    if after_gate is not None:
        after_gate()
    hu = _dot(xb, wu[...])
    act = (_silu(hg) * hu).astype(BF16)
    if after_act is not None:
        after_act()
    return DN_ALPHA * x + 0.5 * _dot(act, wd[...])


def _ffn_pipeline(pre_p, pre_s, rows_p, rows_s, pre_consts, ffn_consts, scratch=(), cast_next=None):
    n_rows_p = rows_p[0].shape[0]
    n_rows_s = rows_s[0].shape[0]
    assert n_rows_p % ROW_TILE == 0
    n_p = n_rows_p // ROW_TILE
    n_in_p, n_in_s, n_pc = len(rows_p), len(rows_s), len(pre_consts)
    consts = [c if isinstance(c, tuple) else (c, _resident(c)) for c in list(pre_consts) + list(ffn_consts)]
    n_cast = 0 if cast_next is None else 3

    def kern(*refs):
        refs = list(refs)
        p_refs, refs = refs[:n_in_p], refs[n_in_p:]
        s_refs, refs = refs[:n_in_s], refs[n_in_s:]
        pc_refs, refs = refs[:n_pc], refs[n_pc:]
        (wg, wu, wd, ln_g, ln_b), refs = refs[:5], refs[5:]
        cast_in, refs = refs[:n_cast], refs[n_cast:]
        (out_p, out_s), refs = refs[:2], refs[2:]
        cast_out, refs = refs[:n_cast], refs[n_cast:]
        (x2_even, x2_odd, z_scr), scr = refs[:3], refs[3:]
        g = pl.program_id(0)
        steady = (g >= 1) & (g <= n_p)


        @pl.when(g == 0)
        def _():
            z_scr[...] = jnp.zeros_like(z_scr)
            x2_even[...] = pre_p(g, [r[...] for r in p_refs], pc_refs, scr)

        def steady_step(x2_cur, x2_next):
            def pre_next():
                x2_next[...] = pre_p(g, [r[...] for r in p_refs], pc_refs, scr)

            def norm_prev():
                out_p[...] = _layer_norm(z_scr[...], ln_g[...], ln_b[...])

            def others():
                norm_prev()
                pre_next()
                for src, dst in zip(cast_in, cast_out):
                    dst[...] = src[...].astype(dst.dtype)

            z_scr[...] = _ffn_residual(x2_cur[...], wg, wu, wd, None, others)

        pl.when(steady & (g % 2 == 1))(lambda: steady_step(x2_even, x2_odd))
        pl.when(steady & (g % 2 == 0))(lambda: steady_step(x2_odd, x2_even))

        @pl.when(g == n_p + 1)
        def _():
            out_p[...] = _layer_norm(z_scr[...], ln_g[...], ln_b[...])
            x2 = pre_s([r[...] for r in s_refs], pc_refs)
            out_s[...] = _layer_norm(_ffn_residual(x2, wg, wu, wd), ln_g[...], ln_b[...])

    tile = lambda cols, lag: pl.BlockSpec(
        (ROW_TILE, cols), lambda g: (jnp.clip(g - lag, 0, n_p - 1), 0))
    in_specs = ([tile(a.shape[1], 0) for a in rows_p]
                + [pl.BlockSpec(a.shape, lambda g: (0, 0)) for a in rows_s]
                + [spec for _, spec in consts])
    out_specs = [tile(D_MODEL, 2), pl.BlockSpec((n_rows_s, D_MODEL), lambda g: (0, 0))]
    out_shape = [jax.ShapeDtypeStruct((n_rows_p, D_MODEL), F32), jax.ShapeDtypeStruct((n_rows_s, D_MODEL), F32)]
    cast_operands = []
    if cast_next is not None:
        *stacked, layer, k = cast_next
        for w in stacked:
            rows, cols = w.shape[2:]
            size = next(s for s in range(BF16_SUBLANES, rows + 1, BF16_SUBLANES)
                        if rows % s == 0 and rows // s <= n_p)
            block = lambda g, last=rows // size - 1: jnp.clip(g - 1, 0, last)
            in_specs.append(pl.BlockSpec((None, None, size, cols), lambda g, b=block: (layer, k, b(g), 0)))
            out_specs.append(pl.BlockSpec((size, cols), lambda g, b=block: (b(g), 0)))
            out_shape.append(jax.ShapeDtypeStruct((rows, cols), BF16))
            cast_operands.append(w)
    return pl.pallas_call(
        kern, grid=(n_p + 2,), in_specs=in_specs, out_specs=out_specs, out_shape=out_shape,
        scratch_shapes=[pltpu.VMEM((ROW_TILE, D_MODEL), F32)] * 3 + list(scratch),
        compiler_params=pltpu.CompilerParams(dimension_semantics=("arbitrary",),
                                             vmem_limit_bytes=VMEM_LIMIT),
    )(*rows_p, *rows_s, *[a for a, _ in consts], *cast_operands)


def _identity_pre(rows, consts):
    (x,) = rows
    return x


def _mix_in_sample_kernel(x_ref, cos_ref, sin_ref, w_ref, o_ref):
    o_ref[...] = _rotary_qk(_dot(x_ref[...].astype(BF16), w_ref[...]), cos_ref[...], sin_ref[...])


def _mix_in_sample(x, cos, sin, w_in):
    full = lambda a: pl.BlockSpec(a.shape, lambda i: (0,) * a.ndim)
    out = jax.ShapeDtypeStruct((x.shape[0], MIX_COLS), F32)
    return pl.pallas_call(
        _mix_in_sample_kernel, grid=(1,), in_specs=[full(x), full(cos), full(sin), w_in[1]],
        out_specs=full(out), out_shape=out,
        compiler_params=pltpu.CompilerParams(vmem_limit_bytes=VMEM_LIMIT),
    )(x, cos, sin, w_in[0])


def _mix_out_ln(rows, consts):
    x, ret, lru = rows
    w_ret, w_lru, g, b = consts
    y = _dot(ret, w_ret[...]) + _dot(lru, w_lru[...])
    return _layer_norm(DN_ALPHA * x + y, g[...], b[...])


def _retention_tables():
    lg = np.log1p(-np.exp2(-5.0 - np.arange(RET_HEADS, dtype=np.float64)))
    idx = np.arange(RET_CHUNK, dtype=np.float64)
    diff = idx[:, None] - idx[None, :]
    decay = np.where(diff >= 0, np.exp(lg[:, None, None] * np.maximum(diff, 0.0)), 0.0)
    ones = np.ones((1, 1, HEAD_DIM))
    q_dec = np.exp(lg[:, None] * (idx + 1.0))[..., None] * ones
    k_dec = np.exp(lg[:, None] * (RET_CHUNK - 1.0 - idx))[..., None] * ones
    c_dec = np.exp(lg * RET_CHUNK)
    as32 = lambda a: jnp.asarray(a.astype(np.float32))
    return as32(decay), as32(q_dec), as32(k_dec), [float(c) for c in c_dec], [float(g) for g in np.exp(lg)]


def _ret_prompt_tile(c_dec, q_ref, k_ref, v_ref, g_ref, gn_ref, dec_ref, qd_ref, kd_ref, o_ref, s_ref):
    heads = range(RET_HEADS)
    head_cols = [slice(h * HEAD_DIM, (h + 1) * HEAD_DIM) for h in heads]
    states = [s_ref[0, h] for h in heads]
    for c in range(SEQ_TILE // RET_CHUNK):
        rows = slice(c * RET_CHUNK, (c + 1) * RET_CHUNK)
        q = [q_ref[rows, cols] for cols in head_cols]
        k = [k_ref[rows, cols] for cols in head_cols]
        vb = [v_ref[rows, cols].astype(BF16) for cols in head_cols]
        scores = [lax.dot_general(q[h].astype(BF16), k[h].astype(BF16), (((1,), (1,)), ((), ())),
                                  preferred_element_type=F32) for h in heads]
        update = [_dot((k[h] * kd_ref[h]).T.astype(BF16), vb[h]) for h in heads]
        for h in heads:
            lhs = jnp.concatenate([(scores[h] * dec_ref[h]).astype(BF16), (q[h] * qd_ref[h]).astype(BF16)], axis=1)
            o = _dot(lhs, jnp.concatenate([vb[h], states[h].astype(BF16)], axis=0))
            states[h] = states[h] * c_dec[h] + update[h]
            o = _unit_norm(o) * gn_ref[:, head_cols[h]]
            o_ref[rows, head_cols[h]] = (_silu(g_ref[rows, head_cols[h]]) * o).astype(o_ref.dtype)
    for h in heads:
        s_ref[0, h] = states[h]


def _ret_sample_kernel(gamma, q_ref, k_ref, v_ref, g_ref, gn_ref, s_in, *rest):
    o_ref, s_out = rest[-2:]
    n = SAMPLE_GROUP
    row = lax.broadcasted_iota(jnp.int32, (n, n * HEAD_DIM), 0)
    blk = lax.broadcasted_iota(jnp.int32, (n, n * HEAD_DIM), 1) // HEAD_DIM
    own = row == blk
    for h in range(RET_HEADS):
        cols = slice(h * HEAD_DIM, (h + 1) * HEAD_DIM)
        q = q_ref[:, cols]
        k = k_ref[:, cols]
        v = v_ref[:, cols]
        v_wide = jnp.where(own, jnp.concatenate([v] * n, axis=1), 0.0).astype(BF16)
        outer = _dot(k.T.astype(BF16), v_wide)
        new = []
        for r in range(n):
            s_new = s_in[r, h] * gamma[h] + outer[:, r * HEAD_DIM:(r + 1) * HEAD_DIM]
            s_out[r, h] = s_new
            new.append(s_new.astype(BF16))
        wide = _dot(q.astype(BF16), jnp.concatenate(new, axis=1))
        wide = jnp.where(own, wide, 0.0)
        o = wide[:, :HEAD_DIM]
        for r in range(1, n):
            o = o + wide[:, r * HEAD_DIM:(r + 1) * HEAD_DIM]
        o = _unit_norm(o) * gn_ref[:, cols]
        o_ref[:, cols] = (_silu(g_ref[:, cols]) * o).astype(o_ref.dtype)


def _ret_sample(proj, gn_g, states, j, new_states):
    _, _, _, _, gamma = _retention_tables()
    n_rows = proj.shape[0]
    n = SAMPLE_GROUP
    col = lambda c: pl.BlockSpec((n, RET_WIDTH), lambda i: (i, c))
    s_spec = pl.BlockSpec((None, n, RET_HEADS, HEAD_DIM, HEAD_DIM), lambda i: (j, i, 0, 0, 0))
    operands = [proj, proj, proj, proj, gn_g, states]
    in_specs = [col(0), col(1), col(2), col(3), _resident(gn_g), s_spec]
    aliases = {}
    if new_states is not None:
        aliases = {len(operands): 1}
        operands.append(new_states)
        in_specs.append(pl.BlockSpec(memory_space=pl.ANY))
    return pl.pallas_call(
        functools.partial(_ret_sample_kernel, gamma),
        grid=(n_rows // n,),
        in_specs=in_specs,
        out_specs=[pl.BlockSpec((n, RET_WIDTH), lambda i: (i, 0)), s_spec],
        out_shape=[jax.ShapeDtypeStruct((n_rows, RET_WIDTH), BF16),
                   jax.ShapeDtypeStruct(states.shape, F32)],
        input_output_aliases=aliases,
        compiler_params=pltpu.CompilerParams(dimension_semantics=("parallel",),
                                             vmem_limit_bytes=VMEM_LIMIT),
    )(*operands)


def _lru_gates(uc, w_gates, b_gates, lam):
    pre = _dot(uc.astype(BF16), w_gates) + b_gates
    r = _sigmoid(pre[:, :LRU_WIDTH])
    i = _sigmoid(pre[:, LRU_WIDTH:])
    neg = -lam
    softplus = jnp.maximum(neg, 0.0) + jnp.log1p(jnp.exp(-jnp.abs(neg)))
    log_a = (-LRU_C * softplus) * r
    a = jnp.exp(log_a)
    xin = jnp.sqrt(jnp.tanh(-log_a) * (1.0 + a * a)) * (i * uc)
    return a, xin


def _lru_prompt_tile(u, ug, cw_ref, cb_ref, wg_ref, bg_ref, lam_ref, h_ref, tail_ref, ext, a_scr, x_scr, h_scr):
    n = SEQ_TILE
    pad = SUBLANES
    ext[pad:pad + n, :] = u
    uc = cb_ref[...] + cw_ref[CONV_W - 1:CONV_W, :] * u
    for j in range(1, CONV_W):
        uc = uc + cw_ref[CONV_W - 1 - j:CONV_W - j, :] * ext[pad - j:pad - j + n, :]
    ext[0:pad, :] = u[n - pad:, :]
    tail_ref[0] = u[n - pad:, :]

    a, x = _lru_gates(uc, wg_ref[...], bg_ref[...], lam_ref[...])

    groups = (n // SUBLANES, SUBLANES, LRU_WIDTH)
    a = a.reshape(groups)
    x = x.reshape(groups)
    step = lax.broadcasted_iota(jnp.int32, groups, 1)
    shift = 1
    while shift < SUBLANES:
        inside = step >= shift
        x = a * jnp.where(inside, pltpu.roll(x, shift, 1), 0.0) + x
        a = a * jnp.where(inside, pltpu.roll(a, shift, 1), 1.0)
        shift *= 2
    a_scr[...] = a.reshape(n, LRU_WIDTH)
    x_scr[...] = x.reshape(n, LRU_WIDTH)

    def group(i, h):
        r = pl.ds(pl.multiple_of(i * SUBLANES, SUBLANES), SUBLANES)
        hg = a_scr[r, :] * h + x_scr[r, :]
        h_scr[r, :] = hg
        return jnp.broadcast_to(hg[SUBLANES - 1:SUBLANES, :], (SUBLANES, LRU_WIDTH))

    h_ref[0] = lax.fori_loop(0, n // SUBLANES, group, h_ref[0], unroll=True)
    return h_scr[...] * _gelu_tanh(ug)


MIX_COLS = 2 * RET_WIDTH * 2 + 2 * LRU_WIDTH
QKVG_COLS = 2 * RET_WIDTH * 2


def _rotary_heads(qk, cos, sin):
    pieces = []
    for j in range(2 * RET_HEADS):
        blk = qk[:, j * HEAD_DIM:(j + 1) * HEAD_DIM]
        rot = blk * cos + pltpu.roll(blk, HEAD_DIM // 2, 1) * sin
        if j < RET_HEADS:
            rot = rot * (HEAD_DIM ** -0.5)
        pieces.append(rot)
    return jnp.concatenate(pieces, axis=1)


def _rotary_qk(proj, cos, sin):
    return jnp.concatenate([_rotary_heads(proj[:, :2 * RET_WIDTH], cos, sin), proj[:, 2 * RET_WIDTH:]], axis=1)


def _mixer_prompt_kernel(c_dec, x_ref, cos_ref, sin_ref, w_in, gn_ref, dec_ref, qd_ref, kd_ref,
                         cw_ref, cb_ref, wg_ref, bg_ref, lam_ref, w_ret, w_lru, ln_g, ln_b,
                         x2_ref, s_ref, h_ref, tail_ref, qk, vg, ret_out, ext, a_scr, x_scr, h_scr):
    @pl.when(pl.program_id(1) == 0)
    def _():
        s_ref[...] = jnp.zeros_like(s_ref)
        h_ref[...] = jnp.zeros_like(h_ref)
        ext[0:SUBLANES, :] = jnp.zeros((SUBLANES, LRU_WIDTH), F32)

    x = x_ref[...]
    xb = x.astype(BF16)
    lru_in = _dot(xb, w_in[:, QKVG_COLS:])
    qk[...] = _rotary_heads(_dot(xb, w_in[:, :2 * RET_WIDTH]), cos_ref[...], sin_ref[...])
    vg[...] = _dot(xb, w_in[:, 2 * RET_WIDTH:QKVG_COLS])
    lru_out = _lru_prompt_tile(lru_in[:, :LRU_WIDTH], lru_in[:, LRU_WIDTH:], cw_ref, cb_ref, wg_ref, bg_ref,
                               lam_ref, h_ref, tail_ref, ext, a_scr, x_scr, h_scr)
    left, right = slice(0, RET_WIDTH), slice(RET_WIDTH, 2 * RET_WIDTH)
    _ret_prompt_tile(c_dec, qk.at[:, left], qk.at[:, right], vg.at[:, left], vg.at[:, right],
                     gn_ref, dec_ref, qd_ref, kd_ref, ret_out, s_ref)
    y = _dot(ret_out[...], w_ret[...]) + _dot(lru_out.astype(BF16), w_lru[...])
    x2_ref[...] = _layer_norm(DN_ALPHA * x + y, ln_g[...], ln_b[...])


def _mixer_prompt(x, cos, sin, consts, batch, seq):
    decay, q_dec, k_dec, c_dec, _ = _retention_tables()
    w_in, gn_g, *rest = consts
    consts = [w_in, gn_g, decay, q_dec, k_dec] + rest
    consts = [c if isinstance(c, tuple) else (c, _resident(c)) for c in consts]
    n_t = seq // SEQ_TILE
    tile = pl.BlockSpec((SEQ_TILE, D_MODEL), lambda b, t: (b * n_t + t, 0))
    table = pl.BlockSpec((SEQ_TILE, HEAD_DIM), lambda b, t: (t, 0))
    small = pl.BlockSpec((1, SUBLANES, LRU_WIDTH), lambda b, t: (b, 0, 0))
    half_f32 = pltpu.VMEM((SEQ_TILE, LRU_WIDTH), F32)
    half_bf16 = pltpu.VMEM((SEQ_TILE, LRU_WIDTH), BF16)
    return pl.pallas_call(
        functools.partial(_mixer_prompt_kernel, c_dec),
        grid=(batch, n_t),
        in_specs=[tile, table, table] + [spec for _, spec in consts],
        out_specs=[tile, pl.BlockSpec((1, RET_HEADS, HEAD_DIM, HEAD_DIM), lambda b, t: (b, 0, 0, 0)),
                   small, small],
        out_shape=[jax.ShapeDtypeStruct((batch * seq, D_MODEL), F32),
                   jax.ShapeDtypeStruct((batch, RET_HEADS, HEAD_DIM, HEAD_DIM), F32),
                   jax.ShapeDtypeStruct((batch, SUBLANES, LRU_WIDTH), F32),
                   jax.ShapeDtypeStruct((batch, SUBLANES, LRU_WIDTH), F32)],
        scratch_shapes=[pltpu.VMEM((SEQ_TILE, 2 * RET_WIDTH), F32), pltpu.VMEM((SEQ_TILE, 2 * RET_WIDTH), F32), half_bf16,
                        pltpu.VMEM((SEQ_TILE + SUBLANES, LRU_WIDTH), F32), half_f32, half_f32, half_f32],
        compiler_params=pltpu.CompilerParams(dimension_semantics=("parallel", "arbitrary"),
                                             vmem_limit_bytes=VMEM_LIMIT),
    )(x, cos, sin, *[a for a, _ in consts])


def _lru_sample_kernel(ux_ref, ug_ref, b0_ref, b1_ref, b2_ref, h0_ref, cw_ref, cb_ref, wg_ref, bg_ref,
                       lam_ref, o_ref, h_ref):
    u = ux_ref[...]
    uc = (cb_ref[...] + cw_ref[3:4, :] * u + cw_ref[2:3, :] * b2_ref[...]
          + cw_ref[1:2, :] * b1_ref[...] + cw_ref[0:1, :] * b0_ref[...])
    a, x = _lru_gates(uc, wg_ref[...], bg_ref[...], lam_ref[...])
    h = a * h0_ref[...] + x
    h_ref[...] = h
    o_ref[...] = (h * _gelu_tanh(ug_ref[...])).astype(o_ref.dtype)


def _lru_sample(proj, conv_buf, h0, conv_w, conv_b, w_gates, b_gates, lam):
    n_rows = proj.shape[0]
    ux_col = 2 * RET_WIDTH * 2 // LRU_WIDTH
    col = lambda j: pl.BlockSpec((n_rows, LRU_WIDTH), lambda i: (0, j))
    bufs = [conv_buf[:, j, :] for j in range(CONV_W - 1)]
    rest = bufs + [h0, conv_w, conv_b, w_gates, b_gates, lam]
    return pl.pallas_call(
        _lru_sample_kernel,
        grid=(1,),
        in_specs=[col(ux_col), col(ux_col + 1)] + [_resident(c) for c in rest],
        out_specs=[pl.BlockSpec((n_rows, LRU_WIDTH), lambda i: (0, 0))] * 2,
        out_shape=[jax.ShapeDtypeStruct((n_rows, LRU_WIDTH), BF16),
                   jax.ShapeDtypeStruct((n_rows, LRU_WIDTH), F32)],
        compiler_params=pltpu.CompilerParams(vmem_limit_bytes=VMEM_LIMIT),
    )(proj, proj, *rest)


def _pool_finish(x, deltas, w_ref, pb_ref, ps_ref, g_ref, b_ref):
    ys = [_dot(d.astype(BF16), w_ref[gi]) for gi, d in enumerate(deltas)]
    y = (jnp.concatenate(ys, axis=1) + pb_ref[...]) * ps_ref[...]
    return _layer_norm(DN_ALPHA * x + y, g_ref[...], b_ref[...])


POOL_HALO = POOL_BUF + 1


def _pool_ln_prompt(tiles_per_seq, step, rows, consts, scratch):
    (x,) = rows
    (ext,) = scratch
    n = ROW_TILE
    t = step % tiles_per_seq

    @pl.when(t == 0)
    def _():
        ext[0:POOL_HALO, :] = jnp.zeros((POOL_HALO, D_MODEL), F32)

    ext[POOL_HALO:POOL_HALO + n, :] = x
    pos = t * n + lax.broadcasted_iota(jnp.int32, (n, POOL_GROUP), 0)
    deltas = []
    for gi, wnd in enumerate(POOL_WINDOWS):
        cols = slice(gi * POOL_GROUP, (gi + 1) * POOL_GROUP)
        w = ext[:, cols]
        shift = 1
        while shift < wnd:
            w = w + pltpu.roll(w, shift, 0)
            shift *= 2
        cnt = jnp.minimum(wnd, pos + 1).astype(F32)
        deltas.append(w[POOL_HALO:, :] / cnt - x[:, cols])
    ext[0:POOL_HALO, :] = x[n - POOL_HALO:, :]
    return _pool_finish(x, deltas, *consts)


def _pool_sample_kernel(x_ref, buf_ref, w_ref, pb_ref, ps_ref, g_ref, b_ref, o_ref):
    x = x_ref[...]
    deltas = []
    for gi, wnd in enumerate(POOL_WINDOWS):
        cols = slice(gi * POOL_GROUP, (gi + 1) * POOL_GROUP)
        s = x[:, cols]
        for j in range(1, wnd):
            s = s + buf_ref[POOL_BUF - j, :, cols]
        cnt = min(float(wnd), PAST_LEN + 1.0)
        deltas.append(s / cnt - x[:, cols])
    o_ref[...] = _pool_finish(x, deltas, w_ref, pb_ref, ps_ref, g_ref, b_ref)


def _pool_sample(x, buf_t, consts):
    full = lambda a: pl.BlockSpec(a.shape, lambda i: (0,) * a.ndim)
    consts = [c if isinstance(c, tuple) else (c, _resident(c)) for c in consts]
    return pl.pallas_call(
        _pool_sample_kernel,
        grid=(1,),
        in_specs=[full(x), full(buf_t)] + [spec for _, spec in consts],
        out_specs=full(x),
        out_shape=jax.ShapeDtypeStruct(x.shape, F32),
        compiler_params=pltpu.CompilerParams(vmem_limit_bytes=VMEM_LIMIT),
    )(x, buf_t, *[a for a, _ in consts])


def _rotary_tables(positions):
    half = HEAD_DIM // 2
    inv = ROPE_BASE ** (-jnp.arange(half, dtype=F32) / half)
    ang = positions.astype(F32)[:, None] * inv[None, :]
    cos, sin = jnp.cos(ang), jnp.sin(ang)
    return jnp.concatenate([cos, cos], axis=1), jnp.concatenate([-sin, sin], axis=1)


def _block_diag(w):
    nb, n, _ = w.shape
    eye = jnp.eye(nb, dtype=w.dtype)
    return (eye[:, None, :, None] * w[:, :, None, :]).reshape(nb * n, nb * n)


def kernel(x_prompt, x_sample, state_ret, state_lru_h, state_lru_conv, state_pool, w_ffn_gate, w_ffn_up, w_ffn_down, ln_g, ln_b, w_mix_in, w_mix_out, ret_gn_g, lru_conv_w, lru_conv_b, lru_wa, lru_ba, lru_wi, lru_bi, lru_lambda, pool_w, pool_b, pool_scale):
    batch, seq, _ = x_prompt.shape
    n_s = x_sample.shape[0]
    assert x_sample.shape[1] == 1 and seq % SEQ_TILE == 0 and n_s % SAMPLE_GROUP == 0
    xp = x_prompt.reshape(batch * seq, D_MODEL)
    xs = x_sample.reshape(n_s, D_MODEL)
    row = lambda a: a.reshape(1, -1)

    cos_p, sin_p = _rotary_tables(jnp.arange(seq))
    cos_s, sin_s = _rotary_tables(jnp.full((n_s,), PAST_LEN))

    w_in_all, w_out_all, pool_w_all = (w.astype(BF16) for w in (w_mix_in, w_mix_out, pool_w))
    ffn_stacks = (w_ffn_gate, w_ffn_up, w_ffn_down)
    ffn_w = [w[0, 0].astype(BF16) for w in ffn_stacks]

    def ffn(pre_p, pre_s, rows_p, rows_s, pre_consts, layer, k, **kw):
        nonlocal ffn_w
        consts = ffn_w + [row(ln_g[layer, 2 * k]), row(ln_b[layer, 2 * k])]
        nxt = (layer, 1) if k == 0 else (layer + 1, 0)
        cast_next = None if nxt[0] == DEPTH else (*ffn_stacks, *nxt)
        xp, xs, *cast = _ffn_pipeline(pre_p, pre_s, rows_p, rows_s, pre_consts, consts,
                                      cast_next=cast_next, **kw)
        ffn_w = cast
        return xp, xs

    rets_p, hs_p, convs_p, pools_p = [], [], [], []
    hs_s, convs_s, pools_s = [], [], []
    rets_s = None
    for layer in range(DEPTH):
        j = layer // 2
        xp, xs = ffn(_same(_identity_pre), _identity_pre, [xp], [xs], [], layer, 0)
        if layer % 2 == 0:
            gn = row(ret_gn_g[j])
            w_gates = jnp.concatenate([_block_diag(lru_wa[j]), _block_diag(lru_wi[j])], axis=1).astype(BF16)
            b_gates = row(jnp.concatenate([lru_ba[j], lru_bi[j]]))
            lru_consts = (lru_conv_w[j], row(lru_conv_b[j]), w_gates, b_gates, row(lru_lambda[j]))
            out_consts = [_pick(w_out_all, j, rows=(RET_WIDTH, 0)), _pick(w_out_all, j, rows=(LRU_WIDTH, 1)),
                          row(ln_g[layer, 1]), row(ln_b[layer, 1])]

            x1_s = xs
            xp, s_p, h_p, tail_p = _mixer_prompt(
                xp, cos_p, sin_p, [_pick(w_in_all, j), gn, *lru_consts, *out_consts], batch, seq)
            proj_s = _mix_in_sample(x1_s, cos_s, sin_s, _pick(w_in_all, j))
            ret_s, rets_s = _ret_sample(proj_s, gn, state_ret, j, rets_s)
            lru_s, h_s = _lru_sample(proj_s, state_lru_conv[j], state_lru_h[j], *lru_consts)

            rets_p.append(s_p)
            hs_p.append(h_p[:, 0, :])
            convs_p.append(tail_p[:, SUBLANES - (CONV_W - 1):, :])
            hs_s.append(h_s)
            ux_s = proj_s[:, 4 * RET_WIDTH:4 * RET_WIDTH + LRU_WIDTH]
            convs_s.append(jnp.concatenate([state_lru_conv[j][:, 1:, :], ux_s[:, None, :]], axis=1))

            xp, xs = ffn(_same(_identity_pre), _mix_out_ln, [xp], [x1_s, ret_s, lru_s], out_consts, layer, 1)
        else:
            pool_consts = [_pick(pool_w_all, j), row(pool_b[j]), row(pool_scale[j]),
                           row(ln_g[layer, 1]), row(ln_b[layer, 1])]
            pools_p.append(xp.reshape(batch, seq, D_MODEL)[:, seq - POOL_BUF:, :])
            pools_s.append(jnp.concatenate([state_pool[j][:, 1:, :], xs[:, None, :]], axis=1))
            xs = _pool_sample(xs, jnp.transpose(state_pool[j], (1, 0, 2)), pool_consts)
            xp, xs = ffn(functools.partial(_pool_ln_prompt, seq // ROW_TILE), lambda rows, consts: rows[0],
                         [xp], [xs], pool_consts, layer, 1,
                         scratch=[pltpu.VMEM((ROW_TILE + POOL_HALO, D_MODEL), F32)])

    return (xp.reshape(batch, seq, D_MODEL), xs.reshape(n_s, 1, D_MODEL),
            jnp.stack(rets_p), jnp.stack(hs_p), jnp.stack(convs_p), jnp.stack(pools_p),
            rets_s, jnp.stack(hs_s), jnp.stack(convs_s), jnp.stack(pools_s))
```

```python
import functools
import math

import numpy as np
import jax
import jax.numpy as jnp
from jax import lax
from jax.experimental import pallas as pl
from jax.experimental.pallas import tpu as pltpu

F32 = jnp.float32
BF16 = jnp.bfloat16

D_MODEL = 1024
DEPTH = 4
PAST_LEN = 16384
RET_HEADS = 4
HEAD_DIM = 128
RET_WIDTH = RET_HEADS * HEAD_DIM
RET_CHUNK = 128
ROPE_BASE = 10000.0
LRU_WIDTH = 512
LRU_C = 8.0
CONV_W = 4
POOL_WINDOWS = (2, 4, 8, 16)
POOL_GROUP = D_MODEL // len(POOL_WINDOWS)
POOL_BUF = max(POOL_WINDOWS) - 1
LN_EPS = 1e-5
DN_ALPHA = (2.0 * DEPTH) ** 0.25

SUBLANES = 8
BF16_SUBLANES = 16
ROW_TILE = 512
SEQ_TILE = 512
SAMPLE_GROUP = 16
CAST_BLOCKS = 16
VMEM_LIMIT = 56 * 1024 * 1024


def _sigmoid(x):
    return 0.5 * jnp.tanh(0.5 * x) + 0.5


def _silu(x):
    return x * _sigmoid(x)


def _gelu_tanh(x):
    return x * (0.5 * (1.0 + jnp.tanh(math.sqrt(2.0 / math.pi) * (x + 0.044715 * (x * x * x)))))


def _layer_norm(z, g, b):
    mu = jnp.mean(z, axis=-1, keepdims=True)
    d = z - mu
    var = jnp.mean(d * d, axis=-1, keepdims=True)
    return d * lax.rsqrt(var + LN_EPS) * g + b


def _unit_norm(z):
    mu = jnp.mean(z, axis=-1, keepdims=True)
    d = z - mu
    var = jnp.mean(d * d, axis=-1, keepdims=True)
    return d * lax.rsqrt(var + LN_EPS)


def _dot(a, b):
    return jnp.dot(a, b, preferred_element_type=F32)


def _resident(arr):
    nd = arr.ndim
    return pl.BlockSpec(arr.shape, lambda *_: (0,) * nd, pipeline_mode=pl.Buffered(1))


def _pick(arr, *lead, rows=None):
    tail = arr.shape[len(lead):]
    first = 0
    if rows is not None:
        tail = (rows[0],) + tail[1:]
        first = rows[1]
    index = tuple(lead) + (first,) + (0,) * (len(tail) - 1)
    return arr, pl.BlockSpec((None,) * len(lead) + tail, lambda *_: index, pipeline_mode=pl.Buffered(1))


def _same(body):
    return lambda step, rows, consts, scratch: body(rows, consts)


def _ffn_residual(x, wg, wu, wd, after_gate=None, after_act=None):
    xb = x.astype(BF16)
    hg = _dot(xb, wg[...])
    if after_gate is not None:
        after_gate()
    hu = _dot(xb, wu[...])
    act = (_silu(hg) * hu).astype(BF16)
    if after_act is not None:
        after_act()
    return DN_ALPHA * x + 0.5 * _dot(act, wd[...])


def _ffn_pipeline(pre_p, pre_s, rows_p, rows_s, pre_consts, ffn_consts, scratch=(), cast_next=None):
    n_rows_p = rows_p[0].shape[0]
    n_rows_s = rows_s[0].shape[0]
    assert n_rows_p % ROW_TILE == 0
    n_p = n_rows_p // ROW_TILE
    n_in_p, n_in_s, n_pc = len(rows_p), len(rows_s), len(pre_consts)
    consts = [c if isinstance(c, tuple) else (c, _resident(c)) for c in list(pre_consts) + list(ffn_consts)]
    n_cast = 0 if cast_next is None else 3

    def kern(*refs):
        refs = list(refs)
        p_refs, refs = refs[:n_in_p], refs[n_in_p:]
        s_refs, refs = refs[:n_in_s], refs[n_in_s:]
        pc_refs, refs = refs[:n_pc], refs[n_pc:]
        (wg, wu, wd, ln_g, ln_b), refs = refs[:5], refs[5:]
        cast_in, refs = refs[:n_cast], refs[n_cast:]
        (out_p, out_s), refs = refs[:2], refs[2:]
        cast_out, refs = refs[:n_cast], refs[n_cast:]
        (x2_even, x2_odd, z_scr), scr = refs[:3], refs[3:]
        g = pl.program_id(0)
        steady = (g >= 1) & (g <= n_p)


        @pl.when(g == 0)
        def _():
            z_scr[...] = jnp.zeros_like(z_scr)
            x2_even[...] = pre_p(g, [r[...] for r in p_refs], pc_refs, scr)

        def steady_step(x2_cur, x2_next):
            def pre_next():
                x2_next[...] = pre_p(g, [r[...] for r in p_refs], pc_refs, scr)

            def norm_prev():
                out_p[...] = _layer_norm(z_scr[...], ln_g[...], ln_b[...])

            def others():
                norm_prev()
                pre_next()
                for src, dst in zip(cast_in, cast_out):
                    dst[...] = src[...].astype(dst.dtype)

            z_scr[...] = _ffn_residual(x2_cur[...], wg, wu, wd, None, others)

        pl.when(steady & (g % 2 == 1))(lambda: steady_step(x2_even, x2_odd))
        pl.when(steady & (g % 2 == 0))(lambda: steady_step(x2_odd, x2_even))

        @pl.when(g == n_p + 1)
        def _():
            out_p[...] = _layer_norm(z_scr[...], ln_g[...], ln_b[...])
            x2 = pre_s([r[...] for r in s_refs], pc_refs)
            out_s[...] = _layer_norm(_ffn_residual(x2, wg, wu, wd), ln_g[...], ln_b[...])

    tile = lambda cols, lag: pl.BlockSpec(
        (ROW_TILE, cols), lambda g: (jnp.clip(g - lag, 0, n_p - 1), 0))
    in_specs = ([tile(a.shape[1], 0) for a in rows_p]
                + [pl.BlockSpec(a.shape, lambda g: (0, 0)) for a in rows_s]
                + [spec for _, spec in consts])
    out_specs = [tile(D_MODEL, 2), pl.BlockSpec((n_rows_s, D_MODEL), lambda g: (0, 0))]
    out_shape = [jax.ShapeDtypeStruct((n_rows_p, D_MODEL), F32), jax.ShapeDtypeStruct((n_rows_s, D_MODEL), F32)]
    cast_operands = []
    if cast_next is not None:
        *stacked, layer, k = cast_next
        for w in stacked:
            rows, cols = w.shape[2:]
            hold = max(1, n_p // CAST_BLOCKS)
            size = next(s for s in range(BF16_SUBLANES, rows + 1, BF16_SUBLANES)
                        if rows % s == 0 and rows // s * hold <= n_p)
            block = lambda g, last=rows // size - 1, hold=hold: jnp.clip((g - 1) // hold, 0, last)
            in_specs.append(pl.BlockSpec((None, None, size, cols), lambda g, b=block: (layer, k, b(g), 0)))
            out_specs.append(pl.BlockSpec((size, cols), lambda g, b=block: (b(g), 0)))
            out_shape.append(jax.ShapeDtypeStruct((rows, cols), BF16))
            cast_operands.append(w)
    return pl.pallas_call(
        kern, grid=(n_p + 2,), in_specs=in_specs, out_specs=out_specs, out_shape=out_shape,
        scratch_shapes=[pltpu.VMEM((ROW_TILE, D_MODEL), F32)] * 3 + list(scratch),
        compiler_params=pltpu.CompilerParams(dimension_semantics=("arbitrary",),
                                             vmem_limit_bytes=VMEM_LIMIT),
    )(*rows_p, *rows_s, *[a for a, _ in consts], *cast_operands)


def _identity_pre(rows, consts):
    (x,) = rows
    return x


def _mix_in_sample_kernel(x_ref, cos_ref, sin_ref, w_ref, o_ref):
    o_ref[...] = _rotary_qk(_dot(x_ref[...].astype(BF16), w_ref[...]), cos_ref[...], sin_ref[...])


def _mix_in_sample(x, cos, sin, w_in):
    full = lambda a: pl.BlockSpec(a.shape, lambda i: (0,) * a.ndim)
    out = jax.ShapeDtypeStruct((x.shape[0], MIX_COLS), F32)
    return pl.pallas_call(
        _mix_in_sample_kernel, grid=(1,), in_specs=[full(x), full(cos), full(sin), w_in[1]],
        out_specs=full(out), out_shape=out,
        compiler_params=pltpu.CompilerParams(vmem_limit_bytes=VMEM_LIMIT),
    )(x, cos, sin, w_in[0])


def _mix_out_ln(rows, consts):
    x, ret, lru = rows
    w_ret, w_lru, g, b = consts
    y = _dot(ret, w_ret[...]) + _dot(lru, w_lru[...])
    return _layer_norm(DN_ALPHA * x + y, g[...], b[...])


def _retention_tables():
    lg = np.log1p(-np.exp2(-5.0 - np.arange(RET_HEADS, dtype=np.float64)))
    idx = np.arange(RET_CHUNK, dtype=np.float64)
    diff = idx[:, None] - idx[None, :]
    decay = np.where(diff >= 0, np.exp(lg[:, None, None] * np.maximum(diff, 0.0)), 0.0)
    ones = np.ones((1, 1, HEAD_DIM))
    q_dec = np.exp(lg[:, None] * (idx + 1.0))[..., None] * ones
    k_dec = np.exp(lg[:, None] * (RET_CHUNK - 1.0 - idx))[..., None] * ones
    c_dec = np.exp(lg * RET_CHUNK)
    as32 = lambda a: jnp.asarray(a.astype(np.float32))
    return as32(decay), as32(q_dec), as32(k_dec), [float(c) for c in c_dec], [float(g) for g in np.exp(lg)]


def _ret_prompt_tile(c_dec, q_ref, k_ref, v_ref, g_ref, gn_ref, dec_ref, qd_ref, kd_ref, o_ref, s_ref):
    heads = range(RET_HEADS)
    head_cols = [slice(h * HEAD_DIM, (h + 1) * HEAD_DIM) for h in heads]
    states = [s_ref[0, h] for h in heads]
    for c in range(SEQ_TILE // RET_CHUNK):
        rows = slice(c * RET_CHUNK, (c + 1) * RET_CHUNK)
        q = [q_ref[rows, cols] for cols in head_cols]
        k = [k_ref[rows, cols] for cols in head_cols]
        vb = [v_ref[rows, cols].astype(BF16) for cols in head_cols]
        scores = [lax.dot_general(q[h].astype(BF16), k[h].astype(BF16), (((1,), (1,)), ((), ())),
                                  preferred_element_type=F32) for h in heads]
        update = [_dot((k[h] * kd_ref[h]).T.astype(BF16), vb[h]) for h in heads]
        for h in heads:
            lhs = jnp.concatenate([(scores[h] * dec_ref[h]).astype(BF16), (q[h] * qd_ref[h]).astype(BF16)], axis=1)
            o = _dot(lhs, jnp.concatenate([vb[h], states[h].astype(BF16)], axis=0))
            states[h] = states[h] * c_dec[h] + update[h]
            o = _unit_norm(o) * gn_ref[:, head_cols[h]]
            o_ref[rows, head_cols[h]] = (_silu(g_ref[rows, head_cols[h]]) * o).astype(o_ref.dtype)
    for h in heads:
        s_ref[0, h] = states[h]


def _ret_sample_kernel(gamma, q_ref, k_ref, v_ref, g_ref, gn_ref, s_in, *rest):
    o_ref, s_out = rest[-2:]
    for earlier in rest[:-2]:
        for slot in range(earlier.shape[0]):
            s_out[slot] = earlier[slot]
    new_slot = s_out.shape[0] - 1
    n = SAMPLE_GROUP
    row = lax.broadcasted_iota(jnp.int32, (n, n * HEAD_DIM), 0)
    blk = lax.broadcasted_iota(jnp.int32, (n, n * HEAD_DIM), 1) // HEAD_DIM
    own = row == blk
    for h in range(RET_HEADS):
        cols = slice(h * HEAD_DIM, (h + 1) * HEAD_DIM)
        q = q_ref[:, cols]
        k = k_ref[:, cols]
        v = v_ref[:, cols]
        v_wide = jnp.where(own, jnp.concatenate([v] * n, axis=1), 0.0).astype(BF16)
        outer = _dot(k.T.astype(BF16), v_wide)
        new = []
        for r in range(n):
            s_new = s_in[r, h] * gamma[h] + outer[:, r * HEAD_DIM:(r + 1) * HEAD_DIM]
            s_out[new_slot, r, h] = s_new
            new.append(s_new.astype(BF16))
        wide = _dot(q.astype(BF16), jnp.concatenate(new, axis=1))
        wide = jnp.where(own, wide, 0.0)
        o = wide[:, :HEAD_DIM]
        for r in range(1, n):
            o = o + wide[:, r * HEAD_DIM:(r + 1) * HEAD_DIM]
        o = _unit_norm(o) * gn_ref[:, cols]
        o_ref[:, cols] = (_silu(g_ref[:, cols]) * o).astype(o_ref.dtype)


def _ret_sample(proj, gn_g, states, j, new_states):
    _, _, _, _, gamma = _retention_tables()
    n_rows = proj.shape[0]
    n = SAMPLE_GROUP
    first = MIX_ORDER.index("q")
    col = lambda c: pl.BlockSpec((n, RET_WIDTH), lambda i: (i, first + c))
    state_block = lambda slots: (slots, n, RET_HEADS, HEAD_DIM, HEAD_DIM)
    operands = [proj, proj, proj, proj, gn_g, states]
    in_specs = [col(0), col(1), col(2), col(3), _resident(gn_g),
                pl.BlockSpec(state_block(None), lambda i: (j, i, 0, 0, 0))]
    if new_states is not None:
        operands.append(new_states)
        in_specs.append(pl.BlockSpec(state_block(j), lambda i: (0, i, 0, 0, 0)))
    return pl.pallas_call(
        functools.partial(_ret_sample_kernel, gamma),
        grid=(n_rows // n,),
        in_specs=in_specs,
        out_specs=[pl.BlockSpec((n, RET_WIDTH), lambda i: (i, 0)),
                   pl.BlockSpec(state_block(j + 1), lambda i: (0, i, 0, 0, 0))],
        out_shape=[jax.ShapeDtypeStruct((n_rows, RET_WIDTH), BF16),
                   jax.ShapeDtypeStruct((j + 1,) + states.shape[1:], F32)],
        compiler_params=pltpu.CompilerParams(dimension_semantics=("parallel",),
                                             vmem_limit_bytes=VMEM_LIMIT),
    )(*operands)


def _lru_gates(uc, w_gates, b_gates, lam):
    pre = _dot(uc.astype(BF16), w_gates) + b_gates
    r = _sigmoid(pre[:, :LRU_WIDTH])
    i = _sigmoid(pre[:, LRU_WIDTH:])
    neg = -lam
    softplus = jnp.maximum(neg, 0.0) + jnp.log1p(jnp.exp(-jnp.abs(neg)))
    log_a = (-LRU_C * softplus) * r
    a = jnp.exp(log_a)
    xin = jnp.sqrt(jnp.tanh(-log_a) * (1.0 + a * a)) * (i * uc)
    return a, xin


def _lru_prompt_tile(u, ug, cw_ref, cb_ref, wg_ref, bg_ref, lam_ref, h_ref, tail_ref, ext, a_scr, x_scr, h_scr):
    n = SEQ_TILE
    pad = SUBLANES
    ext[pad:pad + n, :] = u
    uc = cb_ref[...] + cw_ref[CONV_W - 1:CONV_W, :] * u
    for j in range(1, CONV_W):
        uc = uc + cw_ref[CONV_W - 1 - j:CONV_W - j, :] * ext[pad - j:pad - j + n, :]
    ext[0:pad, :] = u[n - pad:, :]
    tail_ref[0] = u[n - pad:, :]

    a, x = _lru_gates(uc, wg_ref[...], bg_ref[...], lam_ref[...])

    groups = (n // SUBLANES, SUBLANES, LRU_WIDTH)
    a = a.reshape(groups)
    x = x.reshape(groups)
    step = lax.broadcasted_iota(jnp.int32, groups, 1)
    shift = 1
    while shift < SUBLANES:
        inside = step >= shift
        x = a * jnp.where(inside, pltpu.roll(x, shift, 1), 0.0) + x
        a = a * jnp.where(inside, pltpu.roll(a, shift, 1), 1.0)
        shift *= 2
    a_scr[...] = a.reshape(n, LRU_WIDTH)
    x_scr[...] = x.reshape(n, LRU_WIDTH)

    def group(i, h):
        r = pl.ds(pl.multiple_of(i * SUBLANES, SUBLANES), SUBLANES)
        hg = a_scr[r, :] * h + x_scr[r, :]
        h_scr[r, :] = hg
        return jnp.broadcast_to(hg[SUBLANES - 1:SUBLANES, :], (SUBLANES, LRU_WIDTH))

    h_ref[0] = lax.fori_loop(0, n // SUBLANES, group, h_ref[0], unroll=True)
    return h_scr[...] * _gelu_tanh(ug)


QKVG_COLS = 2 * RET_WIDTH * 2
MIX_COLS = QKVG_COLS + 2 * LRU_WIDTH
MIX_ORDER = ("ux", "ug", "q", "k", "v", "g")
QK_START = MIX_ORDER.index("q") * RET_WIDTH
VG_START = MIX_ORDER.index("v") * RET_WIDTH


def _rotary_heads(qk, cos, sin):
    pieces = []
    for j in range(2 * RET_HEADS):
        blk = qk[:, j * HEAD_DIM:(j + 1) * HEAD_DIM]
        rot = blk * cos + pltpu.roll(blk, HEAD_DIM // 2, 1) * sin
        if j < RET_HEADS:
            rot = rot * (HEAD_DIM ** -0.5)
        pieces.append(rot)
    return jnp.concatenate(pieces, axis=1)


def _permute_in_proj(w):
    return jnp.concatenate([w[..., QKVG_COLS:], w[..., :QKVG_COLS]], axis=-1)


def _rotary_qk(proj, cos, sin):
    return jnp.concatenate([proj[:, :QK_START], _rotary_heads(proj[:, QK_START:VG_START], cos, sin),
                            proj[:, VG_START:]], axis=1)


def _mixer_prompt_kernel(c_dec, x_ref, cos_ref, sin_ref, w_in, gn_ref, dec_ref, qd_ref, kd_ref,
                         cw_ref, cb_ref, wg_ref, bg_ref, lam_ref, w_ret, w_lru, ln_g, ln_b,
                         x2_ref, s_ref, h_ref, tail_ref, qk, vg, ret_out, ext, a_scr, x_scr, h_scr):
    @pl.when(pl.program_id(1) == 0)
    def _():
        s_ref[...] = jnp.zeros_like(s_ref)
        h_ref[...] = jnp.zeros_like(h_ref)
        ext[0:SUBLANES, :] = jnp.zeros((SUBLANES, LRU_WIDTH), F32)

    x = x_ref[...]
    xb = x.astype(BF16)
    lru_in = _dot(xb, w_in[:, :QK_START])
    qk_raw = _dot(xb, w_in[:, QK_START:VG_START])
    lru_out = _lru_prompt_tile(lru_in[:, :LRU_WIDTH], lru_in[:, LRU_WIDTH:], cw_ref, cb_ref, wg_ref, bg_ref,
                               lam_ref, h_ref, tail_ref, ext, a_scr, x_scr, h_scr)
    vg[...] = _dot(xb, w_in[:, VG_START:])
    qk[...] = _rotary_heads(qk_raw, cos_ref[...], sin_ref[...])
    left, right = slice(0, RET_WIDTH), slice(RET_WIDTH, 2 * RET_WIDTH)
    _ret_prompt_tile(c_dec, qk.at[:, left], qk.at[:, right], vg.at[:, left], vg.at[:, right],
                     gn_ref, dec_ref, qd_ref, kd_ref, ret_out, s_ref)
    y = _dot(ret_out[...], w_ret[...]) + _dot(lru_out.astype(BF16), w_lru[...])
    x2_ref[...] = _layer_norm(DN_ALPHA * x + y, ln_g[...], ln_b[...])


def _mixer_prompt(x, cos, sin, consts, batch, seq):
    decay, q_dec, k_dec, c_dec, _ = _retention_tables()
    w_in, gn_g, *rest = consts
    consts = [w_in, gn_g, decay, q_dec, k_dec] + rest
    consts = [c if isinstance(c, tuple) else (c, _resident(c)) for c in consts]
    n_t = seq // SEQ_TILE
    tile = pl.BlockSpec((SEQ_TILE, D_MODEL), lambda b, t: (b * n_t + t, 0))
    table = pl.BlockSpec((SEQ_TILE, HEAD_DIM), lambda b, t: (t, 0))
    small = pl.BlockSpec((1, SUBLANES, LRU_WIDTH), lambda b, t: (b, 0, 0))
    half_f32 = pltpu.VMEM((SEQ_TILE, LRU_WIDTH), F32)
    half_bf16 = pltpu.VMEM((SEQ_TILE, LRU_WIDTH), BF16)
    return pl.pallas_call(
        functools.partial(_mixer_prompt_kernel, c_dec),
        grid=(batch, n_t),
        in_specs=[tile, table, table] + [spec for _, spec in consts],
        out_specs=[tile, pl.BlockSpec((1, RET_HEADS, HEAD_DIM, HEAD_DIM), lambda b, t: (b, 0, 0, 0)),
                   small, small],
        out_shape=[jax.ShapeDtypeStruct((batch * seq, D_MODEL), F32),
                   jax.ShapeDtypeStruct((batch, RET_HEADS, HEAD_DIM, HEAD_DIM), F32),
                   jax.ShapeDtypeStruct((batch, SUBLANES, LRU_WIDTH), F32),
                   jax.ShapeDtypeStruct((batch, SUBLANES, LRU_WIDTH), F32)],
        scratch_shapes=[pltpu.VMEM((SEQ_TILE, 2 * RET_WIDTH), F32), pltpu.VMEM((SEQ_TILE, 2 * RET_WIDTH), F32), half_bf16,
                        pltpu.VMEM((SEQ_TILE + SUBLANES, LRU_WIDTH), F32), half_f32, half_f32, half_f32],
        compiler_params=pltpu.CompilerParams(dimension_semantics=("parallel", "arbitrary"),
                                             vmem_limit_bytes=VMEM_LIMIT),
    )(x, cos, sin, *[a for a, _ in consts])


def _lru_sample_kernel(ux_ref, ug_ref, b0_ref, b1_ref, b2_ref, h0_ref, cw_ref, cb_ref, wg_ref, bg_ref,
                       lam_ref, o_ref, h_ref):
    u = ux_ref[...]
    uc = (cb_ref[...] + cw_ref[3:4, :] * u + cw_ref[2:3, :] * b2_ref[...]
          + cw_ref[1:2, :] * b1_ref[...] + cw_ref[0:1, :] * b0_ref[...])
    a, x = _lru_gates(uc, wg_ref[...], bg_ref[...], lam_ref[...])
    h = a * h0_ref[...] + x
    h_ref[...] = h
    o_ref[...] = (h * _gelu_tanh(ug_ref[...])).astype(o_ref.dtype)


def _lru_sample(proj, conv_buf, h0, conv_w, conv_b, w_gates, b_gates, lam):
    n_rows = proj.shape[0]
    ux_col = MIX_ORDER.index("ux")
    col = lambda j: pl.BlockSpec((n_rows, LRU_WIDTH), lambda i: (0, j))
    bufs = [conv_buf[:, j, :] for j in range(CONV_W - 1)]
    rest = bufs + [h0, conv_w, conv_b, w_gates, b_gates, lam]
    return pl.pallas_call(
        _lru_sample_kernel,
        grid=(1,),
        in_specs=[col(ux_col), col(ux_col + 1)] + [_resident(c) for c in rest],
        out_specs=[pl.BlockSpec((n_rows, LRU_WIDTH), lambda i: (0, 0))] * 2,
        out_shape=[jax.ShapeDtypeStruct((n_rows, LRU_WIDTH), BF16),
                   jax.ShapeDtypeStruct((n_rows, LRU_WIDTH), F32)],
        compiler_params=pltpu.CompilerParams(vmem_limit_bytes=VMEM_LIMIT),
    )(proj, proj, *rest)


def _pool_finish(x, deltas, w_ref, pb_ref, ps_ref, g_ref, b_ref):
    ys = [_dot(d.astype(BF16), w_ref[gi]) for gi, d in enumerate(deltas)]
    y = (jnp.concatenate(ys, axis=1) + pb_ref[...]) * ps_ref[...]
    return _layer_norm(DN_ALPHA * x + y, g_ref[...], b_ref[...])


POOL_HALO = POOL_BUF + 1


def _pool_ln_prompt(tiles_per_seq, step, rows, consts, scratch):
    (x,) = rows
    (ext,) = scratch
    n = ROW_TILE
    t = step % tiles_per_seq

    @pl.when(t == 0)
    def _():
        ext[0:POOL_HALO, :] = jnp.zeros((POOL_HALO, D_MODEL), F32)

    ext[POOL_HALO:POOL_HALO + n, :] = x
    pos = t * n + lax.broadcasted_iota(jnp.int32, (n, POOL_GROUP), 0)
    deltas = []
    for gi, wnd in enumerate(POOL_WINDOWS):
        cols = slice(gi * POOL_GROUP, (gi + 1) * POOL_GROUP)
        w = ext[:, cols]
        shift = 1
        while shift < wnd:
            w = w + pltpu.roll(w, shift, 0)
            shift *= 2
        cnt = jnp.minimum(wnd, pos + 1).astype(F32)
        deltas.append(w[POOL_HALO:, :] / cnt - x[:, cols])
    ext[0:POOL_HALO, :] = x[n - POOL_HALO:, :]
    return _pool_finish(x, deltas, *consts)


def _pool_sample_kernel(x_ref, buf_ref, w_ref, pb_ref, ps_ref, g_ref, b_ref, o_ref):
    x = x_ref[...]
    deltas = []
    for gi, wnd in enumerate(POOL_WINDOWS):
        cols = slice(gi * POOL_GROUP, (gi + 1) * POOL_GROUP)
        s = x[:, cols]
        for j in range(1, wnd):
            s = s + buf_ref[POOL_BUF - j, :, cols]
        cnt = min(float(wnd), PAST_LEN + 1.0)
        deltas.append(s / cnt - x[:, cols])
    o_ref[...] = _pool_finish(x, deltas, w_ref, pb_ref, ps_ref, g_ref, b_ref)


def _pool_sample(x, buf_t, consts):
    full = lambda a: pl.BlockSpec(a.shape, lambda i: (0,) * a.ndim)
    consts = [c if isinstance(c, tuple) else (c, _resident(c)) for c in consts]
    return pl.pallas_call(
        _pool_sample_kernel,
        grid=(1,),
        in_specs=[full(x), full(buf_t)] + [spec for _, spec in consts],
        out_specs=full(x),
        out_shape=jax.ShapeDtypeStruct(x.shape, F32),
        compiler_params=pltpu.CompilerParams(vmem_limit_bytes=VMEM_LIMIT),
    )(x, buf_t, *[a for a, _ in consts])


def _rotary_tables(positions):
    half = HEAD_DIM // 2
    inv = ROPE_BASE ** (-jnp.arange(half, dtype=F32) / half)
    ang = positions.astype(F32)[:, None] * inv[None, :]
    cos, sin = jnp.cos(ang), jnp.sin(ang)
    return jnp.concatenate([cos, cos], axis=1), jnp.concatenate([-sin, sin], axis=1)


def _block_diag(w):
    nb, n, _ = w.shape
    eye = jnp.eye(nb, dtype=w.dtype)
    return (eye[:, None, :, None] * w[:, :, None, :]).reshape(nb * n, nb * n)


def kernel(x_prompt, x_sample, state_ret, state_lru_h, state_lru_conv, state_pool, w_ffn_gate, w_ffn_up, w_ffn_down, ln_g, ln_b, w_mix_in, w_mix_out, ret_gn_g, lru_conv_w, lru_conv_b, lru_wa, lru_ba, lru_wi, lru_bi, lru_lambda, pool_w, pool_b, pool_scale):
    batch, seq, _ = x_prompt.shape
    n_s = x_sample.shape[0]
    assert x_sample.shape[1] == 1 and seq % SEQ_TILE == 0 and n_s % SAMPLE_GROUP == 0
    xp = x_prompt.reshape(batch * seq, D_MODEL)
    xs = x_sample.reshape(n_s, D_MODEL)
    row = lambda a: a.reshape(1, -1)

    cos_p, sin_p = _rotary_tables(jnp.arange(seq))
    cos_s, sin_s = _rotary_tables(jnp.full((n_s,), PAST_LEN))

    w_in_all, w_out_all, pool_w_all = (w.astype(BF16) for w in (_permute_in_proj(w_mix_in), w_mix_out, pool_w))
    ffn_stacks = (w_ffn_gate, w_ffn_up, w_ffn_down)
    ffn_w = [w[0, 0].astype(BF16) for w in ffn_stacks]

    def ffn(pre_p, pre_s, rows_p, rows_s, pre_consts, layer, k, **kw):
        nonlocal ffn_w
        consts = ffn_w + [row(ln_g[layer, 2 * k]), row(ln_b[layer, 2 * k])]
        nxt = (layer, 1) if k == 0 else (layer + 1, 0)
        cast_next = None if nxt[0] == DEPTH else (*ffn_stacks, *nxt)
        xp, xs, *cast = _ffn_pipeline(pre_p, pre_s, rows_p, rows_s, pre_consts, consts,
                                      cast_next=cast_next, **kw)
        ffn_w = cast
        return xp, xs

    rets_p, hs_p, convs_p, pools_p = [], [], [], []
    hs_s, convs_s, pools_s = [], [], []
    rets_s = None
    for layer in range(DEPTH):
        j = layer // 2
        xp, xs = ffn(_same(_identity_pre), _identity_pre, [xp], [xs], [], layer, 0)
        if layer % 2 == 0:
            gn = row(ret_gn_g[j])
            w_gates = jnp.concatenate([_block_diag(lru_wa[j]), _block_diag(lru_wi[j])], axis=1).astype(BF16)
            b_gates = row(jnp.concatenate([lru_ba[j], lru_bi[j]]))
            lru_consts = (lru_conv_w[j], row(lru_conv_b[j]), w_gates, b_gates, row(lru_lambda[j]))
            out_consts = [_pick(w_out_all, j, rows=(RET_WIDTH, 0)), _pick(w_out_all, j, rows=(LRU_WIDTH, 1)),
                          row(ln_g[layer, 1]), row(ln_b[layer, 1])]

            x1_s = xs
            xp, s_p, h_p, tail_p = _mixer_prompt(
                xp, cos_p, sin_p, [_pick(w_in_all, j), gn, *lru_consts, *out_consts], batch, seq)
            proj_s = _mix_in_sample(x1_s, cos_s, sin_s, _pick(w_in_all, j))
            ret_s, rets_s = _ret_sample(proj_s, gn, state_ret, j, rets_s)
            lru_s, h_s = _lru_sample(proj_s, state_lru_conv[j], state_lru_h[j], *lru_consts)

            rets_p.append(s_p)
            hs_p.append(h_p[:, 0, :])
            convs_p.append(tail_p[:, SUBLANES - (CONV_W - 1):, :])
            hs_s.append(h_s)
            ux_s = proj_s[:, MIX_ORDER.index("ux") * LRU_WIDTH:][:, :LRU_WIDTH]
            convs_s.append(jnp.concatenate([state_lru_conv[j][:, 1:, :], ux_s[:, None, :]], axis=1))

            xp, xs = ffn(_same(_identity_pre), _mix_out_ln, [xp], [x1_s, ret_s, lru_s], out_consts, layer, 1)
        else:
            pool_consts = [_pick(pool_w_all, j), row(pool_b[j]), row(pool_scale[j]),
                           row(ln_g[layer, 1]), row(ln_b[layer, 1])]
            pools_p.append(xp.reshape(batch, seq, D_MODEL)[:, seq - POOL_BUF:, :])
            pools_s.append(jnp.concatenate([state_pool[j][:, 1:, :], xs[:, None, :]], axis=1))
            xs = _pool_sample(xs, jnp.transpose(state_pool[j], (1, 0, 2)), pool_consts)
            xp, xs = ffn(functools.partial(_pool_ln_prompt, seq // ROW_TILE), lambda rows, consts: rows[0],
                         [xp], [xs], pool_consts, layer, 1,
                         scratch=[pltpu.VMEM((ROW_TILE + POOL_HALO, D_MODEL), F32)])

    return (xp.reshape(batch, seq, D_MODEL), xs.reshape(n_s, 1, D_MODEL),
            jnp.stack(rets_p), jnp.stack(hs_p), jnp.stack(convs_p), jnp.stack(pools_p),
            rets_s, jnp.stack(hs_s), jnp.stack(convs_s), jnp.stack(pools_s))
```

```python
import functools
import math

import numpy as np
import jax
import jax.numpy as jnp
from jax import lax
from jax.experimental import pallas as pl
from jax.experimental.pallas import tpu as pltpu

F32 = jnp.float32
BF16 = jnp.bfloat16

D_MODEL = 1024
DEPTH = 4
PAST_LEN = 16384
RET_HEADS = 4
HEAD_DIM = 128
RET_WIDTH = RET_HEADS * HEAD_DIM
RET_CHUNK = 128
ROPE_BASE = 10000.0
LRU_WIDTH = 512
LRU_C = 8.0
CONV_W = 4
POOL_WINDOWS = (2, 4, 8, 16)
POOL_GROUP = D_MODEL // len(POOL_WINDOWS)
POOL_BUF = max(POOL_WINDOWS) - 1
LN_EPS = 1e-5
DN_ALPHA = (2.0 * DEPTH) ** 0.25

SUBLANES = 8
BF16_SUBLANES = 16
ROW_TILE = 512
SEQ_TILE = 512
SAMPLE_GROUP = 16
CAST_BLOCKS = 32
VMEM_LIMIT = 56 * 1024 * 1024


def _sigmoid(x):
    return 0.5 * jnp.tanh(0.5 * x) + 0.5


def _silu(x):
    return x * _sigmoid(x)


def _gelu_tanh(x):
    return x * (0.5 * (1.0 + jnp.tanh(math.sqrt(2.0 / math.pi) * (x + 0.044715 * (x * x * x)))))


def _layer_norm(z, g, b):
    mu = jnp.mean(z, axis=-1, keepdims=True)
    d = z - mu
    var = jnp.mean(d * d, axis=-1, keepdims=True)
    return d * lax.rsqrt(var + LN_EPS) * g + b


def _unit_norm(z):
    mu = jnp.mean(z, axis=-1, keepdims=True)
    d = z - mu
    var = jnp.mean(d * d, axis=-1, keepdims=True)
    return d * lax.rsqrt(var + LN_EPS)


def _dot(a, b):
    return jnp.dot(a, b, preferred_element_type=F32)


def _resident(arr):
    nd = arr.ndim
    return pl.BlockSpec(arr.shape, lambda *_: (0,) * nd, pipeline_mode=pl.Buffered(1))


def _pick(arr, *lead, rows=None):
    tail = arr.shape[len(lead):]
    first = 0
    if rows is not None:
        tail = (rows[0],) + tail[1:]
        first = rows[1]
    index = tuple(lead) + (first,) + (0,) * (len(tail) - 1)
    return arr, pl.BlockSpec((None,) * len(lead) + tail, lambda *_: index, pipeline_mode=pl.Buffered(1))


def _same(body):
    return lambda tile, rows, consts, scratch, parity, first: (body([r[...] for r in rows], consts), ())


def _ffn_residual(x, wg, wu, wd, between=()):
    between = list(between) + [None, None]
    xb = x.astype(BF16)
    hg = _dot(xb, wg[...])
    if between[0] is not None:
        between[0]()
    hu = _dot(xb, wu[...])
    act = (_silu(hg) * hu).astype(BF16)
    if between[1] is not None:
        between[1]()
    return DN_ALPHA * x + 0.5 * _dot(act, wd[...])


def _ffn_pipeline(pre_p, pre_s, rows_p, rows_s, pre_consts, ffn_consts, scratch=(), cast_next=None,
                  row_leads=None):
    n_rows_p = rows_p[0].shape[0]
    n_rows_s = rows_s[0].shape[0]
    assert n_rows_p % ROW_TILE == 0
    n_p = n_rows_p // ROW_TILE
    n_in_p, n_in_s, n_pc = len(rows_p), len(rows_s), len(pre_consts)
    consts = [c if isinstance(c, tuple) else (c, _resident(c)) for c in list(pre_consts) + list(ffn_consts)]
    n_cast = 0 if cast_next is None else 3

    def kern(*refs):
        refs = list(refs)
        p_refs, refs = refs[:n_in_p], refs[n_in_p:]
        s_refs, refs = refs[:n_in_s], refs[n_in_s:]
        pc_refs, refs = refs[:n_pc], refs[n_pc:]
        (wg, wu, wd, ln_g, ln_b), refs = refs[:5], refs[5:]
        cast_in, refs = refs[:n_cast], refs[n_cast:]
        (out_p, out_s), refs = refs[:2], refs[2:]
        cast_out, refs = refs[:n_cast], refs[n_cast:]
        (x2_even, x2_odd, z_scr), scr = refs[:3], refs[3:]
        g = pl.program_id(0)
        steady = (g >= 1) & (g <= n_p)


        @pl.when(g == 0)
        def _():
            z_scr[...] = jnp.zeros_like(z_scr)
            x2, later = pre_p(g, p_refs, pc_refs, scr, 0, True)
            x2_even[...] = x2
            for work in later:
                work()

        def steady_step(x2_cur, x2_next, parity):
            x2_main = x2_cur[...]
            x2, later = pre_p(g, p_refs, pc_refs, scr, parity, False)
            x2_next[...] = x2
            later = list(later) + [lambda: None] * 2

            def after_gate():
                later[0]()

            def before_down():
                later[1]()
                out_p[...] = _layer_norm(z_scr[...], ln_g[...], ln_b[...])
                for src, dst in zip(cast_in, cast_out):
                    dst[...] = src[...].astype(dst.dtype)

            z_scr[...] = _ffn_residual(x2_main, wg, wu, wd, (after_gate, before_down))

        pl.when(steady & (g % 2 == 1))(lambda: steady_step(x2_even, x2_odd, 1))
        pl.when(steady & (g % 2 == 0))(lambda: steady_step(x2_odd, x2_even, 0))

        @pl.when(g == n_p + 1)
        def _():
            out_p[...] = _layer_norm(z_scr[...], ln_g[...], ln_b[...])
            x2 = pre_s([r[...] for r in s_refs], pc_refs)
            out_s[...] = _layer_norm(_ffn_residual(x2, wg, wu, wd), ln_g[...], ln_b[...])

    tile = lambda cols, lag: pl.BlockSpec(
        (ROW_TILE, cols), lambda g: (jnp.clip(g - lag, 0, n_p - 1), 0))
    in_specs = ([tile(a.shape[1], -lead) for a, lead in zip(rows_p, row_leads or [0] * n_in_p)]
                + [pl.BlockSpec(a.shape, lambda g: (0, 0)) for a in rows_s]
                + [spec for _, spec in consts])
    out_specs = [tile(D_MODEL, 2), pl.BlockSpec((n_rows_s, D_MODEL), lambda g: (0, 0))]
    out_shape = [jax.ShapeDtypeStruct((n_rows_p, D_MODEL), F32), jax.ShapeDtypeStruct((n_rows_s, D_MODEL), F32)]
    cast_operands = []
    if cast_next is not None:
        *stacked, layer, k = cast_next
        for w in stacked:
            rows, cols = w.shape[2:]
            hold = max(1, n_p // CAST_BLOCKS)
            size = next(s for s in range(BF16_SUBLANES, rows + 1, BF16_SUBLANES)
                        if rows % s == 0 and rows // s * hold <= n_p)
            block = lambda g, last=rows // size - 1, hold=hold: jnp.clip((g - 1) // hold, 0, last)
            in_specs.append(pl.BlockSpec((None, None, size, cols), lambda g, b=block: (layer, k, b(g), 0)))
            out_specs.append(pl.BlockSpec((size, cols), lambda g, b=block: (b(g), 0)))
            out_shape.append(jax.ShapeDtypeStruct((rows, cols), BF16))
            cast_operands.append(w)
    return pl.pallas_call(
        kern, grid=(n_p + 2,), in_specs=in_specs, out_specs=out_specs, out_shape=out_shape,
        scratch_shapes=[pltpu.VMEM((ROW_TILE, D_MODEL), F32)] * 3 + list(scratch),
        compiler_params=pltpu.CompilerParams(dimension_semantics=("arbitrary",),
                                             vmem_limit_bytes=VMEM_LIMIT),
    )(*rows_p, *rows_s, *[a for a, _ in consts], *cast_operands)


def _identity_pre(rows, consts):
    (x,) = rows
    return x


def _mix_in_sample_kernel(x_ref, cos_ref, sin_ref, w_ref, o_ref):
    o_ref[...] = _rotary_qk(_dot(x_ref[...].astype(BF16), w_ref[...]), cos_ref[...], sin_ref[...])


def _mix_in_sample(x, cos, sin, w_in):
    full = lambda a: pl.BlockSpec(a.shape, lambda i: (0,) * a.ndim)
    out = jax.ShapeDtypeStruct((x.shape[0], MIX_COLS), F32)
    return pl.pallas_call(
        _mix_in_sample_kernel, grid=(1,), in_specs=[full(x), full(cos), full(sin), w_in[1]],
        out_specs=full(out), out_shape=out,
        compiler_params=pltpu.CompilerParams(vmem_limit_bytes=VMEM_LIMIT),
    )(x, cos, sin, w_in[0])


def _mix_out_ln(rows, consts):
    x, ret, lru = rows
    w_ret, w_lru, g, b = consts
    y = _dot(ret, w_ret[...]) + _dot(lru, w_lru[...])
    return _layer_norm(DN_ALPHA * x + y, g[...], b[...])


def _retention_tables():
    lg = np.log1p(-np.exp2(-5.0 - np.arange(RET_HEADS, dtype=np.float64)))
    idx = np.arange(RET_CHUNK, dtype=np.float64)
    diff = idx[:, None] - idx[None, :]
    decay = np.where(diff >= 0, np.exp(lg[:, None, None] * np.maximum(diff, 0.0)), 0.0)
    ones = np.ones((1, 1, HEAD_DIM))
    q_dec = np.exp(lg[:, None] * (idx + 1.0))[..., None] * ones
    k_dec = np.exp(lg[:, None] * (RET_CHUNK - 1.0 - idx))[..., None] * ones
    c_dec = np.exp(lg * RET_CHUNK)
    as32 = lambda a: jnp.asarray(a.astype(np.float32))
    return as32(decay), as32(q_dec), as32(k_dec), [float(c) for c in c_dec], [float(g) for g in np.exp(lg)]


def _ret_prompt_tile(c_dec, q_ref, k_ref, v_ref, g_ref, gn_ref, dec_ref, qd_ref, kd_ref, s_ref):
    heads = range(RET_HEADS)
    head_cols = [slice(h * HEAD_DIM, (h + 1) * HEAD_DIM) for h in heads]
    states = [s_ref[0, h] for h in heads]
    out_rows = []
    for c in range(SEQ_TILE // RET_CHUNK):
        rows = slice(c * RET_CHUNK, (c + 1) * RET_CHUNK)
        out_heads = []
        q = [q_ref[rows, cols] for cols in head_cols]
        k = [k_ref[rows, cols] for cols in head_cols]
        vb = [v_ref[rows, cols].astype(BF16) for cols in head_cols]
        scores = [lax.dot_general(q[h].astype(BF16), k[h].astype(BF16), (((1,), (1,)), ((), ())),
                                  preferred_element_type=F32) for h in heads]
        update = [_dot((k[h] * kd_ref[h]).T.astype(BF16), vb[h]) for h in heads]
        for h in heads:
            lhs = jnp.concatenate([(scores[h] * dec_ref[h]).astype(BF16), (q[h] * qd_ref[h]).astype(BF16)], axis=1)
            o = _dot(lhs, jnp.concatenate([vb[h], states[h].astype(BF16)], axis=0))
            states[h] = states[h] * c_dec[h] + update[h]
            o = _unit_norm(o) * gn_ref[:, head_cols[h]]
            out_heads.append((_silu(g_ref[rows, head_cols[h]]) * o).astype(BF16))
        out_rows.append(jnp.concatenate(out_heads, axis=1))
    for h in heads:
        s_ref[0, h] = states[h]
    return jnp.concatenate(out_rows, axis=0)


def _ret_sample_kernel(gamma, q_ref, k_ref, v_ref, g_ref, gn_ref, s_in, *rest):
    earlier, outs = rest[:len(rest) // 3 * 3], rest[len(rest) // 3 * 3:]
    o_ref = outs[0]
    s_out = outs[1] if len(outs) > 1 else None
    n = SAMPLE_GROUP
    row = lax.broadcasted_iota(jnp.int32, (n, n * HEAD_DIM), 0)
    blk = lax.broadcasted_iota(jnp.int32, (n, n * HEAD_DIM), 1) // HEAD_DIM
    own = row == blk

    def new_states(k_ref, v_ref, s_ref, h):
        cols = slice(h * HEAD_DIM, (h + 1) * HEAD_DIM)
        v_wide = jnp.where(own, jnp.concatenate([v_ref[:, cols]] * n, axis=1), 0.0).astype(BF16)
        outer = _dot(k_ref[:, cols].T.astype(BF16), v_wide)
        return [s_ref[r, h] * gamma[h] + outer[:, r * HEAD_DIM:(r + 1) * HEAD_DIM] for r in range(n)]

    for h in range(RET_HEADS):
        cols = slice(h * HEAD_DIM, (h + 1) * HEAD_DIM)
        for slot in range(len(earlier) // 3):
            for r, s_new in enumerate(new_states(*earlier[3 * slot:3 * slot + 3], h)):
                s_out[slot, r, h] = s_new
        new = new_states(k_ref, v_ref, s_in, h)
        if s_out is not None:
            for r, s_new in enumerate(new):
                s_out[len(earlier) // 3, r, h] = s_new
        q = q_ref[:, cols]
        wide = _dot(q.astype(BF16), jnp.concatenate([s.astype(BF16) for s in new], axis=1))
        wide = jnp.where(own, wide, 0.0)
        o = wide[:, :HEAD_DIM]
        for r in range(1, n):
            o = o + wide[:, r * HEAD_DIM:(r + 1) * HEAD_DIM]
        o = _unit_norm(o) * gn_ref[:, cols]
        o_ref[:, cols] = (_silu(g_ref[:, cols]) * o).astype(o_ref.dtype)


def _ret_sample(projs, gn_g, states, write_states):
    _, _, _, _, gamma = _retention_tables()
    j = len(projs) - 1
    n_rows = projs[-1].shape[0]
    n = SAMPLE_GROUP
    col = lambda name: pl.BlockSpec((n, RET_WIDTH), lambda i: (i, MIX_ORDER.index(name)))
    state = lambda slot: pl.BlockSpec((None, n, RET_HEADS, HEAD_DIM, HEAD_DIM), lambda i: (slot, i, 0, 0, 0))
    operands = [projs[-1]] * 4 + [gn_g, states]
    in_specs = [col("q"), col("k"), col("v"), col("g"), _resident(gn_g), state(j)]
    out_specs = [pl.BlockSpec((n, RET_WIDTH), lambda i: (i, 0))]
    out_shape = [jax.ShapeDtypeStruct((n_rows, RET_WIDTH), BF16)]
    if write_states:
        for slot, proj in enumerate(projs[:-1]):
            operands += [proj, proj, states]
            in_specs += [col("k"), col("v"), state(slot)]
        out_specs.append(pl.BlockSpec((j + 1, n, RET_HEADS, HEAD_DIM, HEAD_DIM), lambda i: (0, i, 0, 0, 0)))
        out_shape.append(jax.ShapeDtypeStruct((j + 1,) + states.shape[1:], F32))
    return pl.pallas_call(
        functools.partial(_ret_sample_kernel, gamma),
        grid=(n_rows // n,),
        in_specs=in_specs, out_specs=out_specs, out_shape=out_shape,
        compiler_params=pltpu.CompilerParams(dimension_semantics=("parallel",),
                                             vmem_limit_bytes=VMEM_LIMIT),
    )(*operands)


def _lru_gates(uc, w_gates, b_gates, lam):
    pre = _dot(uc.astype(BF16), w_gates) + b_gates
    r = _sigmoid(pre[:, :LRU_WIDTH])
    i = _sigmoid(pre[:, LRU_WIDTH:])
    neg = -lam
    softplus = jnp.maximum(neg, 0.0) + jnp.log1p(jnp.exp(-jnp.abs(neg)))
    log_a = (-LRU_C * softplus) * r
    a = jnp.exp(log_a)
    xin = jnp.sqrt(jnp.tanh(-log_a) * (1.0 + a * a)) * (i * uc)
    return a, xin


def _lru_prompt_tile(u, ug, cw_ref, cb_ref, wg_ref, bg_ref, lam_ref, h_ref, tail_ref, ext):
    n = SEQ_TILE
    pad = SUBLANES
    ext[pad:pad + n, :] = u
    uc = cb_ref[...] + cw_ref[CONV_W - 1:CONV_W, :] * u
    for j in range(1, CONV_W):
        uc = uc + cw_ref[CONV_W - 1 - j:CONV_W - j, :] * ext[pad - j:pad - j + n, :]
    ext[0:pad, :] = u[n - pad:, :]
    tail_ref[0] = u[n - pad:, :]

    a, x = _lru_gates(uc, wg_ref[...], bg_ref[...], lam_ref[...])

    groups = (n // SUBLANES, SUBLANES, LRU_WIDTH)
    a = a.reshape(groups)
    x = x.reshape(groups)
    step = lax.broadcasted_iota(jnp.int32, groups, 1)
    shift = 1
    while shift < SUBLANES:
        inside = step >= shift
        x = a * jnp.where(inside, pltpu.roll(x, shift, 1), 0.0) + x
        a = a * jnp.where(inside, pltpu.roll(a, shift, 1), 1.0)
        shift *= 2
    h = h_ref[0]
    hs = []
    for i in range(n // SUBLANES):
        hg = a[i] * h + x[i]
        hs.append(hg)
        h = jnp.broadcast_to(hg[SUBLANES - 1:SUBLANES, :], (SUBLANES, LRU_WIDTH))
    h_ref[0] = h
    return jnp.concatenate(hs, axis=0) * _gelu_tanh(ug)


QKVG_COLS = 2 * RET_WIDTH * 2
MIX_COLS = QKVG_COLS + 2 * LRU_WIDTH
MIX_ORDER = ("q", "k", "v", "g", "ux", "ug")
QK_START = MIX_ORDER.index("q") * RET_WIDTH
VG_START = MIX_ORDER.index("v") * RET_WIDTH
LRU_START = MIX_ORDER.index("ux") * RET_WIDTH


def _rotary_heads(qk, cos, sin):
    pieces = []
    for j in range(2 * RET_HEADS):
        blk = qk[:, j * HEAD_DIM:(j + 1) * HEAD_DIM]
        rot = blk * cos + pltpu.roll(blk, HEAD_DIM // 2, 1) * sin
        if j < RET_HEADS:
            rot = rot * (HEAD_DIM ** -0.5)
        pieces.append(rot)
    return jnp.concatenate(pieces, axis=1)


def _rotary_qk(proj, cos, sin):
    assert QK_START == 0
    return jnp.concatenate([_rotary_heads(proj[:, :VG_START], cos, sin), proj[:, VG_START:]], axis=1)


def _mixer_prompt_kernel(c_dec, x_ref, cos_ref, sin_ref, w_in, gn_ref, dec_ref, qd_ref, kd_ref,
                         cw_ref, cb_ref, wg_ref, bg_ref, lam_ref, w_ret, w_lru, ln_g, ln_b,
                         x2_ref, s_ref, h_ref, tail_ref, ext):
    @pl.when(pl.program_id(1) == 0)
    def _():
        s_ref[...] = jnp.zeros_like(s_ref)
        h_ref[...] = jnp.zeros_like(h_ref)
        ext[0:SUBLANES, :] = jnp.zeros((SUBLANES, LRU_WIDTH), F32)

    x = x_ref[...]
    xb = x.astype(BF16)
    lru_in = _dot(xb, w_in[:, LRU_START:])
    qk_raw = _dot(xb, w_in[:, QK_START:VG_START])
    lru_out = _lru_prompt_tile(lru_in[:, :LRU_WIDTH], lru_in[:, LRU_WIDTH:], cw_ref, cb_ref, wg_ref, bg_ref,
                               lam_ref, h_ref, tail_ref, ext)
    vg = _dot(xb, w_in[:, VG_START:LRU_START])
    qk = _rotary_heads(qk_raw, cos_ref[...], sin_ref[...])
    ret_out = _ret_prompt_tile(c_dec, qk[:, :RET_WIDTH], qk[:, RET_WIDTH:], vg[:, :RET_WIDTH], vg[:, RET_WIDTH:],
                               gn_ref, dec_ref, qd_ref, kd_ref, s_ref)
    y = _dot(ret_out, w_ret[...]) + _dot(lru_out.astype(BF16), w_lru[...])
    x2_ref[...] = _layer_norm(DN_ALPHA * x + y, ln_g[...], ln_b[...])


def _mixer_prompt(x, cos, sin, consts, batch, seq):
    decay, q_dec, k_dec, c_dec, _ = _retention_tables()
    w_in, gn_g, *rest = consts
    consts = [w_in, gn_g, decay, q_dec, k_dec] + rest
    consts = [c if isinstance(c, tuple) else (c, _resident(c)) for c in consts]
    n_t = seq // SEQ_TILE
    tile = pl.BlockSpec((SEQ_TILE, D_MODEL), lambda b, t: (b * n_t + t, 0))
    table = pl.BlockSpec((SEQ_TILE, HEAD_DIM), lambda b, t: (t, 0))
    small = pl.BlockSpec((1, SUBLANES, LRU_WIDTH), lambda b, t: (b, 0, 0))
    return pl.pallas_call(
        functools.partial(_mixer_prompt_kernel, c_dec),
        grid=(batch, n_t),
        in_specs=[tile, table, table] + [spec for _, spec in consts],
        out_specs=[tile, pl.BlockSpec((1, RET_HEADS, HEAD_DIM, HEAD_DIM), lambda b, t: (b, 0, 0, 0)),
                   small, small],
        out_shape=[jax.ShapeDtypeStruct((batch * seq, D_MODEL), F32),
                   jax.ShapeDtypeStruct((batch, RET_HEADS, HEAD_DIM, HEAD_DIM), F32),
                   jax.ShapeDtypeStruct((batch, SUBLANES, LRU_WIDTH), F32),
                   jax.ShapeDtypeStruct((batch, SUBLANES, LRU_WIDTH), F32)],
        scratch_shapes=[pltpu.VMEM((SEQ_TILE + SUBLANES, LRU_WIDTH), F32)],
        compiler_params=pltpu.CompilerParams(dimension_semantics=("parallel", "arbitrary"),
                                             vmem_limit_bytes=VMEM_LIMIT),
    )(x, cos, sin, *[a for a, _ in consts])


def _lru_sample_kernel(ux_ref, ug_ref, b0_ref, b1_ref, b2_ref, h0_ref, cw_ref, cb_ref, wg_ref, bg_ref,
                       lam_ref, o_ref, h_ref):
    u = ux_ref[...]
    uc = (cb_ref[...] + cw_ref[3:4, :] * u + cw_ref[2:3, :] * b2_ref[...]
          + cw_ref[1:2, :] * b1_ref[...] + cw_ref[0:1, :] * b0_ref[...])
    a, x = _lru_gates(uc, wg_ref[...], bg_ref[...], lam_ref[...])
    h = a * h0_ref[...] + x
    h_ref[...] = h
    o_ref[...] = (h * _gelu_tanh(ug_ref[...])).astype(o_ref.dtype)


def _lru_sample(proj, conv_buf, h0, conv_w, conv_b, w_gates, b_gates, lam):
    n_rows = proj.shape[0]
    ux_col = MIX_ORDER.index("ux")
    col = lambda j: pl.BlockSpec((n_rows, LRU_WIDTH), lambda i: (0, j))
    bufs = [conv_buf[:, j, :] for j in range(CONV_W - 1)]
    rest = bufs + [h0, conv_w, conv_b, w_gates, b_gates, lam]
    return pl.pallas_call(
        _lru_sample_kernel,
        grid=(1,),
        in_specs=[col(ux_col), col(ux_col + 1)] + [_resident(c) for c in rest],
        out_specs=[pl.BlockSpec((n_rows, LRU_WIDTH), lambda i: (0, 0))] * 2,
        out_shape=[jax.ShapeDtypeStruct((n_rows, LRU_WIDTH), BF16),
                   jax.ShapeDtypeStruct((n_rows, LRU_WIDTH), F32)],
        compiler_params=pltpu.CompilerParams(vmem_limit_bytes=VMEM_LIMIT),
    )(proj, proj, *rest)


def _pool_finish(x, deltas, w_ref, pb_ref, ps_ref, g_ref, b_ref):
    ys = [_dot(deltas[:, gi * POOL_GROUP:(gi + 1) * POOL_GROUP], w_ref[gi]) for gi in range(len(POOL_WINDOWS))]
    y = (jnp.concatenate(ys, axis=1) + pb_ref[...]) * ps_ref[...]
    return _layer_norm(DN_ALPHA * x + y, g_ref[...], b_ref[...])


POOL_HALO = POOL_BUF + 1


POOL_ROWS = 64


def _pool_deltas(tiles_per_seq, tile, x_ref, halo, out, groups):
    n = ROW_TILE
    t = tile % tiles_per_seq
    for gi in groups:
        wnd = POOL_WINDOWS[gi]
        cols = slice(gi * POOL_GROUP, (gi + 1) * POOL_GROUP)
        for r0 in range(0, n, POOL_ROWS):
            rows = slice(r0, r0 + POOL_ROWS)
            if r0 == 0:
                before = jnp.where(t == 0, 0.0, halo[:, cols])
                w = jnp.concatenate([before, x_ref[rows, cols]], axis=0)
            else:
                w = x_ref[r0 - POOL_HALO:r0 + POOL_ROWS, cols]
            shift = 1
            while shift < wnd:
                w = w + pltpu.roll(w, shift, 0)
                shift *= 2
            pos = t * n + r0 + lax.broadcasted_iota(jnp.int32, (POOL_ROWS, POOL_GROUP), 0)
            cnt = jnp.minimum(wnd, pos + 1).astype(F32)
            out[rows, cols] = (w[POOL_HALO:, :] / cnt - x_ref[rows, cols]).astype(BF16)
        halo[:, cols] = x_ref[n - POOL_HALO:, cols]


def _pool_ln_prompt(tiles_per_seq, tile, rows, consts, scratch, parity, first):
    x_ref, x_next_ref = rows
    halo, stash_even, stash_odd = scratch
    stash = (stash_even, stash_odd)
    all_groups = range(len(POOL_WINDOWS))
    if first:
        halo[...] = jnp.zeros_like(halo)
        _pool_deltas(tiles_per_seq, tile, x_ref, halo, stash[parity], all_groups)
    x2 = _pool_finish(x_ref[...], stash[parity][...], *consts)
    ahead = functools.partial(_pool_deltas, tiles_per_seq, tile + 1, x_next_ref, halo, stash[1 - parity])
    return x2, (lambda: ahead(all_groups[2:]), lambda: ahead(all_groups[:2]))


def _pool_sample_kernel(x_ref, buf_ref, w_ref, pb_ref, ps_ref, g_ref, b_ref, o_ref):
    x = x_ref[...]
    deltas = []
    for gi, wnd in enumerate(POOL_WINDOWS):
        cols = slice(gi * POOL_GROUP, (gi + 1) * POOL_GROUP)
        s = x[:, cols]
        for j in range(1, wnd):
            s = s + buf_ref[POOL_BUF - j, :, cols]
        cnt = min(float(wnd), PAST_LEN + 1.0)
        deltas.append((s / cnt - x[:, cols]).astype(BF16))
    o_ref[...] = _pool_finish(x, jnp.concatenate(deltas, axis=1), w_ref, pb_ref, ps_ref, g_ref, b_ref)


def _pool_sample(x, buf_t, consts):
    full = lambda a: pl.BlockSpec(a.shape, lambda i: (0,) * a.ndim)
    consts = [c if isinstance(c, tuple) else (c, _resident(c)) for c in consts]
    return pl.pallas_call(
        _pool_sample_kernel,
        grid=(1,),
        in_specs=[full(x), full(buf_t)] + [spec for _, spec in consts],
        out_specs=full(x),
        out_shape=jax.ShapeDtypeStruct(x.shape, F32),
        compiler_params=pltpu.CompilerParams(vmem_limit_bytes=VMEM_LIMIT),
    )(x, buf_t, *[a for a, _ in consts])


def _rotary_tables(positions):
    half = HEAD_DIM // 2
    inv = ROPE_BASE ** (-jnp.arange(half, dtype=F32) / half)
    ang = positions.astype(F32)[:, None] * inv[None, :]
    cos, sin = jnp.cos(ang), jnp.sin(ang)
    return jnp.concatenate([cos, cos], axis=1), jnp.concatenate([-sin, sin], axis=1)


def _block_diag(w):
    nb, n, _ = w.shape
    eye = jnp.eye(nb, dtype=w.dtype)
    return (eye[:, None, :, None] * w[:, :, None, :]).reshape(nb * n, nb * n)


def kernel(x_prompt, x_sample, state_ret, state_lru_h, state_lru_conv, state_pool, w_ffn_gate, w_ffn_up, w_ffn_down, ln_g, ln_b, w_mix_in, w_mix_out, ret_gn_g, lru_conv_w, lru_conv_b, lru_wa, lru_ba, lru_wi, lru_bi, lru_lambda, pool_w, pool_b, pool_scale):
    batch, seq, _ = x_prompt.shape
    n_s = x_sample.shape[0]
    assert x_sample.shape[1] == 1 and seq % SEQ_TILE == 0 and n_s % SAMPLE_GROUP == 0
    xp = x_prompt.reshape(batch * seq, D_MODEL)
    xs = x_sample.reshape(n_s, D_MODEL)
    row = lambda a: a.reshape(1, -1)

    cos_p, sin_p = _rotary_tables(jnp.arange(seq))
    cos_s, sin_s = _rotary_tables(jnp.full((n_s,), PAST_LEN))

    w_in_all, w_out_all, pool_w_all = (w.astype(BF16) for w in (w_mix_in, w_mix_out, pool_w))
    ffn_stacks = (w_ffn_gate, w_ffn_up, w_ffn_down)
    ffn_w = [w[0, 0].astype(BF16) for w in ffn_stacks]

    def ffn(pre_p, pre_s, rows_p, rows_s, pre_consts, layer, k, **kw):
        nonlocal ffn_w
        consts = ffn_w + [row(ln_g[layer, 2 * k]), row(ln_b[layer, 2 * k])]
        nxt = (layer, 1) if k == 0 else (layer + 1, 0)
        cast_next = None if nxt[0] == DEPTH else (*ffn_stacks, *nxt)
        xp, xs, *cast = _ffn_pipeline(pre_p, pre_s, rows_p, rows_s, pre_consts, consts,
                                      cast_next=cast_next, **kw)
        ffn_w = cast
        return xp, xs

    rets_p, hs_p, convs_p, pools_p = [], [], [], []
    hs_s, convs_s, pools_s = [], [], []
    projs_s = []
    for layer in range(DEPTH):
        j = layer // 2
        xp, xs = ffn(_same(_identity_pre), _identity_pre, [xp], [xs], [], layer, 0)
        if layer % 2 == 0:
            gn = row(ret_gn_g[j])
            w_gates = jnp.concatenate([_block_diag(lru_wa[j]), _block_diag(lru_wi[j])], axis=1).astype(BF16)
            b_gates = row(jnp.concatenate([lru_ba[j], lru_bi[j]]))
            lru_consts = (lru_conv_w[j], row(lru_conv_b[j]), w_gates, b_gates, row(lru_lambda[j]))
            out_consts = [_pick(w_out_all, j, rows=(RET_WIDTH, 0)), _pick(w_out_all, j, rows=(LRU_WIDTH, 1)),
                          row(ln_g[layer, 1]), row(ln_b[layer, 1])]

            x1_s = xs
            xp, s_p, h_p, tail_p = _mixer_prompt(
                xp, cos_p, sin_p, [_pick(w_in_all, j), gn, *lru_consts, *out_consts], batch, seq)
            proj_s = _mix_in_sample(x1_s, cos_s, sin_s, _pick(w_in_all, j))
            projs_s.append(proj_s)
            ret_s, *rets_s = _ret_sample(projs_s, gn, state_ret, write_states=len(projs_s) == state_ret.shape[0])
            lru_s, h_s = _lru_sample(proj_s, state_lru_conv[j], state_lru_h[j], *lru_consts)

            rets_p.append(s_p)
            hs_p.append(h_p[:, 0, :])
            convs_p.append(tail_p[:, SUBLANES - (CONV_W - 1):, :])
            hs_s.append(h_s)
            ux_s = proj_s[:, MIX_ORDER.index("ux") * LRU_WIDTH:][:, :LRU_WIDTH]
            convs_s.append(jnp.concatenate([state_lru_conv[j][:, 1:, :], ux_s[:, None, :]], axis=1))

            xp, xs = ffn(_same(_identity_pre), _mix_out_ln, [xp], [x1_s, ret_s, lru_s], out_consts, layer, 1)
        else:
            pool_consts = [_pick(pool_w_all, j), row(pool_b[j]), row(pool_scale[j]),
                           row(ln_g[layer, 1]), row(ln_b[layer, 1])]
            pools_p.append(xp.reshape(batch, seq, D_MODEL)[:, seq - POOL_BUF:, :])
            pools_s.append(jnp.concatenate([state_pool[j][:, 1:, :], xs[:, None, :]], axis=1))
            xs = _pool_sample(xs, jnp.transpose(state_pool[j], (1, 0, 2)), pool_consts)
            xp, xs = ffn(functools.partial(_pool_ln_prompt, seq // ROW_TILE), lambda rows, consts: rows[0],
                         [xp, xp], [xs], pool_consts, layer, 1, row_leads=[0, 1],
                         scratch=[pltpu.VMEM((POOL_HALO, D_MODEL), F32)]
                         + [pltpu.VMEM((ROW_TILE, D_MODEL), BF16)] * 2)

    return (xp.reshape(batch, seq, D_MODEL), xs.reshape(n_s, 1, D_MODEL),
            jnp.stack(rets_p), jnp.stack(hs_p), jnp.stack(convs_p), jnp.stack(pools_p),
            rets_s[0], jnp.stack(hs_s), jnp.stack(convs_s), jnp.stack(pools_s))
```

```python
import functools
import math

import numpy as np
import jax
import jax.numpy as jnp
from jax import lax
from jax.experimental import pallas as pl
from jax.experimental.pallas import tpu as pltpu

F32 = jnp.float32
BF16 = jnp.bfloat16

D_MODEL = 1024
DEPTH = 4
PAST_LEN = 16384
RET_HEADS = 4
HEAD_DIM = 128
RET_WIDTH = RET_HEADS * HEAD_DIM
RET_CHUNK = 128
ROPE_BASE = 10000.0
LRU_WIDTH = 512
LRU_C = 8.0
CONV_W = 4
POOL_WINDOWS = (2, 4, 8, 16)
POOL_GROUP = D_MODEL // len(POOL_WINDOWS)
POOL_BUF = max(POOL_WINDOWS) - 1
LN_EPS = 1e-5
DN_ALPHA = (2.0 * DEPTH) ** 0.25

SUBLANES = 8
BF16_SUBLANES = 16
ROW_TILE = 512
SEQ_TILE = 512
SAMPLE_GROUP = 16
CAST_BLOCKS = 32
VMEM_LIMIT = 56 * 1024 * 1024


def _sigmoid(x):
    return 0.5 * jnp.tanh(0.5 * x) + 0.5


def _silu(x):
    return x * _sigmoid(x)


def _gelu_tanh(x):
    return x * (0.5 * (1.0 + jnp.tanh(math.sqrt(2.0 / math.pi) * (x + 0.044715 * (x * x * x)))))


def _layer_norm(z, g, b):
    mu = jnp.mean(z, axis=-1, keepdims=True)
    d = z - mu
    var = jnp.mean(d * d, axis=-1, keepdims=True)
    return d * lax.rsqrt(var + LN_EPS) * g + b


def _unit_norm(z):
    mu = jnp.mean(z, axis=-1, keepdims=True)
    d = z - mu
    var = jnp.mean(d * d, axis=-1, keepdims=True)
    return d * lax.rsqrt(var + LN_EPS)


def _dot(a, b):
    return jnp.dot(a, b, preferred_element_type=F32)


def _resident(arr):
    nd = arr.ndim
    return pl.BlockSpec(arr.shape, lambda *_: (0,) * nd, pipeline_mode=pl.Buffered(1))


def _with_spec(const):
    return const if isinstance(const, tuple) else (const, _resident(const))


def _pick(arr, *lead, rows=None):
    tail = arr.shape[len(lead):]
    first = 0
    if rows is not None:
        tail = (rows[0],) + tail[1:]
        first = rows[1]
    index = tuple(lead) + (first,) + (0,) * (len(tail) - 1)
    return arr, pl.BlockSpec((None,) * len(lead) + tail, lambda *_: index, pipeline_mode=pl.Buffered(1))


def _same(body):
    return lambda tile, rows, consts, scratch, parity, first: (body([r[...] for r in rows], consts), ())


def _ffn_residual(x, wg, wu, wd, between=()):
    between = list(between) + [None, None]
    xb = x.astype(BF16)
    hg = _dot(xb, wg[...])
    if between[0] is not None:
        between[0]()
    hu = _dot(xb, wu[...])
    act = (_silu(hg) * hu).astype(BF16)
    if between[1] is not None:
        between[1]()
    return DN_ALPHA * x + 0.5 * _dot(act, wd[...])


def _ffn_pipeline(pre_p, pre_s, rows_p, rows_s, pre_consts, ffn_consts, scratch=(), cast_next=None,
                  row_leads=None):
    n_rows_p = rows_p[0].shape[0]
    n_rows_s = rows_s[0].shape[0]
    assert n_rows_p % ROW_TILE == 0
    n_p = n_rows_p // ROW_TILE
    n_in_p, n_in_s, n_pc = len(rows_p), len(rows_s), len(pre_consts)
    consts = [_with_spec(c) for c in list(pre_consts) + list(ffn_consts)]
    n_cast = 0 if cast_next is None else 3

    def kern(*refs):
        refs = list(refs)
        p_refs, refs = refs[:n_in_p], refs[n_in_p:]
        s_refs, refs = refs[:n_in_s], refs[n_in_s:]
        pc_refs, refs = refs[:n_pc], refs[n_pc:]
        (wg, wu, wd, ln_g, ln_b), refs = refs[:5], refs[5:]
        cast_in, refs = refs[:n_cast], refs[n_cast:]
        (out_p, out_s), refs = refs[:2], refs[2:]
        cast_out, refs = refs[:n_cast], refs[n_cast:]
        (x2_even, x2_odd, z_scr), scr = refs[:3], refs[3:]
        g = pl.program_id(0)
        steady = (g >= 1) & (g <= n_p)


        @pl.when(g == 0)
        def _():
            z_scr[...] = jnp.zeros_like(z_scr)
            x2, later = pre_p(g, p_refs, pc_refs, scr, 0, True)
            x2_even[...] = x2
            for work in later:
                work()

        def steady_step(x2_cur, x2_next, parity):
            later = []

            def after_gate():
                x2, work = pre_p(g, p_refs, pc_refs, scr, parity, False)
                x2_next[...] = x2
                later.extend(work)
                if later:
                    later.pop(0)()

            def before_down():
                if later:
                    later.pop(0)()
                out_p[...] = _layer_norm(z_scr[...], ln_g[...], ln_b[...])
                for src, dst in zip(cast_in, cast_out):
                    dst[...] = src[...].astype(dst.dtype)

            z_scr[...] = _ffn_residual(x2_cur[...], wg, wu, wd, (after_gate, before_down))

        pl.when(steady & (g % 2 == 1))(lambda: steady_step(x2_even, x2_odd, 1))
        pl.when(steady & (g % 2 == 0))(lambda: steady_step(x2_odd, x2_even, 0))

        @pl.when(g == n_p + 1)
        def _():
            out_p[...] = _layer_norm(z_scr[...], ln_g[...], ln_b[...])
            x2 = pre_s([r[...] for r in s_refs], pc_refs)
            out_s[...] = _layer_norm(_ffn_residual(x2, wg, wu, wd), ln_g[...], ln_b[...])

    tile = lambda cols, lag: pl.BlockSpec(
        (ROW_TILE, cols), lambda g: (jnp.clip(g - lag, 0, n_p - 1), 0))
    in_specs = ([tile(a.shape[1], -lead) for a, lead in zip(rows_p, row_leads or [0] * n_in_p)]
                + [pl.BlockSpec(a.shape, lambda g: (0, 0)) for a in rows_s]
                + [spec for _, spec in consts])
    out_specs = [tile(D_MODEL, 2), pl.BlockSpec((n_rows_s, D_MODEL), lambda g: (0, 0))]
    out_shape = [jax.ShapeDtypeStruct((n_rows_p, D_MODEL), F32), jax.ShapeDtypeStruct((n_rows_s, D_MODEL), F32)]
    cast_operands = []
    if cast_next is not None:
        *stacked, layer, k = cast_next
        for w in stacked:
            rows, cols = w.shape[2:]
            hold = max(1, n_p // CAST_BLOCKS)
            size = next(s for s in range(BF16_SUBLANES, rows + 1, BF16_SUBLANES)
                        if rows % s == 0 and rows // s * hold <= n_p)
            block = lambda g, last=rows // size - 1, hold=hold: jnp.clip((g - 1) // hold, 0, last)
            in_specs.append(pl.BlockSpec((None, None, size, cols), lambda g, b=block: (layer, k, b(g), 0)))
            out_specs.append(pl.BlockSpec((size, cols), lambda g, b=block: (b(g), 0)))
            out_shape.append(jax.ShapeDtypeStruct((rows, cols), BF16))
            cast_operands.append(w)
    return pl.pallas_call(
        kern, grid=(n_p + 2,), in_specs=in_specs, out_specs=out_specs, out_shape=out_shape,
        scratch_shapes=[pltpu.VMEM((ROW_TILE, D_MODEL), F32)] * 3 + list(scratch),
        compiler_params=pltpu.CompilerParams(dimension_semantics=("arbitrary",),
                                             vmem_limit_bytes=VMEM_LIMIT),
    )(*rows_p, *rows_s, *[a for a, _ in consts], *cast_operands)


def _identity_pre(rows, consts):
    (x,) = rows
    return x


def _mix_in_sample_kernel(x_ref, cos_ref, sin_ref, w_ref, o_ref):
    o_ref[...] = _rotary_qk(_dot(x_ref[...].astype(BF16), w_ref[...]), cos_ref[...], sin_ref[...])


def _mix_in_sample(x, cos, sin, w_in):
    full = lambda a: pl.BlockSpec(a.shape, lambda i: (0,) * a.ndim)
    out = jax.ShapeDtypeStruct((x.shape[0], MIX_COLS), F32)
    return pl.pallas_call(
        _mix_in_sample_kernel, grid=(1,), in_specs=[full(x), full(cos), full(sin), w_in[1]],
        out_specs=full(out), out_shape=out,
        compiler_params=pltpu.CompilerParams(vmem_limit_bytes=VMEM_LIMIT),
    )(x, cos, sin, w_in[0])


def _mix_out_ln(rows, consts):
    x, ret, lru = rows
    w_ret, w_lru, g, b = consts
    y = _dot(ret, w_ret[...]) + _dot(lru, w_lru[...])
    return _layer_norm(DN_ALPHA * x + y, g[...], b[...])


def _retention_tables():
    lg = np.log1p(-np.exp2(-5.0 - np.arange(RET_HEADS, dtype=np.float64)))
    idx = np.arange(RET_CHUNK, dtype=np.float64)
    diff = idx[:, None] - idx[None, :]
    decay = np.where(diff >= 0, np.exp(lg[:, None, None] * np.maximum(diff, 0.0)), 0.0)
    ones = np.ones((1, 1, HEAD_DIM))
    q_dec = np.exp(lg[:, None] * (idx + 1.0))[..., None] * ones
    k_dec = np.exp(lg[:, None] * (RET_CHUNK - 1.0 - idx))[..., None] * ones
    c_dec = np.exp(lg * RET_CHUNK)
    as32 = lambda a: jnp.asarray(a.astype(np.float32))
    return as32(decay), as32(q_dec), as32(k_dec), [float(c) for c in c_dec], [float(g) for g in np.exp(lg)]


def _ret_prompt_tile(c_dec, q_ref, k_ref, v_ref, g_ref, gn_ref, dec_ref, qd_ref, kd_ref, s_ref):
    heads = range(RET_HEADS)
    head_cols = [slice(h * HEAD_DIM, (h + 1) * HEAD_DIM) for h in heads]
    states = [s_ref[0, h] for h in heads]
    out_rows = []
    for c in range(SEQ_TILE // RET_CHUNK):
        rows = slice(c * RET_CHUNK, (c + 1) * RET_CHUNK)
        out_heads = []
        q = [q_ref[rows, cols] for cols in head_cols]
        k = [k_ref[rows, cols] for cols in head_cols]
        vb = [v_ref[rows, cols].astype(BF16) for cols in head_cols]
        scores = [lax.dot_general(q[h].astype(BF16), k[h].astype(BF16), (((1,), (1,)), ((), ())),
                                  preferred_element_type=F32) for h in heads]
        update = [_dot((k[h] * kd_ref[h]).T.astype(BF16), vb[h]) for h in heads]
        for h in heads:
            lhs = jnp.concatenate([(scores[h] * dec_ref[h]).astype(BF16), (q[h] * qd_ref[h]).astype(BF16)], axis=1)
            o = _dot(lhs, jnp.concatenate([vb[h], states[h].astype(BF16)], axis=0))
            states[h] = states[h] * c_dec[h] + update[h]
            o = _unit_norm(o) * gn_ref[:, head_cols[h]]
            out_heads.append((_silu(g_ref[rows, head_cols[h]]) * o).astype(BF16))
        out_rows.append(jnp.concatenate(out_heads, axis=1))
    for h in heads:
        s_ref[0, h] = states[h]
    return jnp.concatenate(out_rows, axis=0)


def _ret_sample_kernel(gamma, q_ref, k_ref, v_ref, g_ref, gn_ref, s_in, *rest):
    earlier, outs = rest[:len(rest) // 3 * 3], rest[len(rest) // 3 * 3:]
    o_ref = outs[0]
    s_out = outs[1] if len(outs) > 1 else None
    n = SAMPLE_GROUP
    row = lax.broadcasted_iota(jnp.int32, (n, n * HEAD_DIM), 0)
    blk = lax.broadcasted_iota(jnp.int32, (n, n * HEAD_DIM), 1) // HEAD_DIM
    own = row == blk

    def new_states(k_ref, v_ref, s_ref, h):
        cols = slice(h * HEAD_DIM, (h + 1) * HEAD_DIM)
        v_wide = jnp.where(own, jnp.concatenate([v_ref[:, cols]] * n, axis=1), 0.0).astype(BF16)
        outer = _dot(k_ref[:, cols].T.astype(BF16), v_wide)
        return [s_ref[r, h] * gamma[h] + outer[:, r * HEAD_DIM:(r + 1) * HEAD_DIM] for r in range(n)]

    for h in range(RET_HEADS):
        cols = slice(h * HEAD_DIM, (h + 1) * HEAD_DIM)
        for slot in range(len(earlier) // 3):
            for r, s_new in enumerate(new_states(*earlier[3 * slot:3 * slot + 3], h)):
                s_out[slot, r, h] = s_new
        new = new_states(k_ref, v_ref, s_in, h)
        if s_out is not None:
            for r, s_new in enumerate(new):
                s_out[len(earlier) // 3, r, h] = s_new
        q = q_ref[:, cols]
        wide = _dot(q.astype(BF16), jnp.concatenate([s.astype(BF16) for s in new], axis=1))
        wide = jnp.where(own, wide, 0.0)
        o = wide[:, :HEAD_DIM]
        for r in range(1, n):
            o = o + wide[:, r * HEAD_DIM:(r + 1) * HEAD_DIM]
        o = _unit_norm(o) * gn_ref[:, cols]
        o_ref[:, cols] = (_silu(g_ref[:, cols]) * o).astype(o_ref.dtype)


def _ret_sample(projs, gn_g, states, write_states):
    _, _, _, _, gamma = _retention_tables()
    j = len(projs) - 1
    n_rows = projs[-1].shape[0]
    n = SAMPLE_GROUP
    col = lambda name: pl.BlockSpec((n, RET_WIDTH), lambda i: (i, MIX_ORDER.index(name)))
    state = lambda slot: pl.BlockSpec((None, n, RET_HEADS, HEAD_DIM, HEAD_DIM), lambda i: (slot, i, 0, 0, 0))
    gn_g, gn_spec = _with_spec(gn_g)
    operands = [projs[-1]] * 4 + [gn_g, states]
    in_specs = [col("q"), col("k"), col("v"), col("g"), gn_spec, state(j)]
    out_specs = [pl.BlockSpec((n, RET_WIDTH), lambda i: (i, 0))]
    out_shape = [jax.ShapeDtypeStruct((n_rows, RET_WIDTH), BF16)]
    if write_states:
        for slot, proj in enumerate(projs[:-1]):
            operands += [proj, proj, states]
            in_specs += [col("k"), col("v"), state(slot)]
        out_specs.append(pl.BlockSpec((j + 1, n, RET_HEADS, HEAD_DIM, HEAD_DIM), lambda i: (0, i, 0, 0, 0)))
        out_shape.append(jax.ShapeDtypeStruct((j + 1,) + states.shape[1:], F32))
    return pl.pallas_call(
        functools.partial(_ret_sample_kernel, gamma),
        grid=(n_rows // n,),
        in_specs=in_specs, out_specs=out_specs, out_shape=out_shape,
        compiler_params=pltpu.CompilerParams(dimension_semantics=("parallel",),
                                             vmem_limit_bytes=VMEM_LIMIT),
    )(*operands)


def _lru_gates(uc, w_gates, b_gates, lam):
    pre = _dot(uc.astype(BF16), w_gates) + b_gates
    r = _sigmoid(pre[:, :LRU_WIDTH])
    i = _sigmoid(pre[:, LRU_WIDTH:])
    neg = -lam
    softplus = jnp.maximum(neg, 0.0) + jnp.log1p(jnp.exp(-jnp.abs(neg)))
    log_a = (-LRU_C * softplus) * r
    a = jnp.exp(log_a)
    xin = jnp.sqrt(jnp.tanh(-log_a) * (1.0 + a * a)) * (i * uc)
    return a, xin


def _lru_prompt_tile(u, ug, cw_ref, cb_ref, wg_ref, bg_ref, lam_ref, h_ref, tail_ref, ext):
    n = SEQ_TILE
    pad = SUBLANES
    ext[pad:pad + n, :] = u
    uc = cb_ref[...] + cw_ref[CONV_W - 1:CONV_W, :] * u
    for j in range(1, CONV_W):
        uc = uc + cw_ref[CONV_W - 1 - j:CONV_W - j, :] * ext[pad - j:pad - j + n, :]
    ext[0:pad, :] = u[n - pad:, :]
    tail_ref[0] = u[n - pad:, :]

    a, x = _lru_gates(uc, wg_ref[...], bg_ref[...], lam_ref[...])

    groups = (n // SUBLANES, SUBLANES, LRU_WIDTH)
    a = a.reshape(groups)
    x = x.reshape(groups)
    step = lax.broadcasted_iota(jnp.int32, groups, 1)
    shift = 1
    while shift < SUBLANES:
        inside = step >= shift
        x = a * jnp.where(inside, pltpu.roll(x, shift, 1), 0.0) + x
        a = a * jnp.where(inside, pltpu.roll(a, shift, 1), 1.0)
        shift *= 2
    h = h_ref[0]
    hs = []
    for i in range(n // SUBLANES):
        hg = a[i] * h + x[i]
        hs.append(hg)
        h = jnp.broadcast_to(hg[SUBLANES - 1:SUBLANES, :], (SUBLANES, LRU_WIDTH))
    h_ref[0] = h
    return jnp.concatenate(hs, axis=0) * _gelu_tanh(ug)


QKVG_COLS = 2 * RET_WIDTH * 2
MIX_COLS = QKVG_COLS + 2 * LRU_WIDTH
MIX_ORDER = ("q", "k", "v", "g", "ux", "ug")
QK_START = MIX_ORDER.index("q") * RET_WIDTH
VG_START = MIX_ORDER.index("v") * RET_WIDTH
LRU_START = MIX_ORDER.index("ux") * RET_WIDTH


def _rotary_heads(qk, cos, sin):
    pieces = []
    for j in range(2 * RET_HEADS):
        blk = qk[:, j * HEAD_DIM:(j + 1) * HEAD_DIM]
        rot = blk * cos + pltpu.roll(blk, HEAD_DIM // 2, 1) * sin
        if j < RET_HEADS:
            rot = rot * (HEAD_DIM ** -0.5)
        pieces.append(rot)
    return jnp.concatenate(pieces, axis=1)


def _rotary_qk(proj, cos, sin):
    assert QK_START == 0
    return jnp.concatenate([_rotary_heads(proj[:, :VG_START], cos, sin), proj[:, VG_START:]], axis=1)


def _mixer_prompt_kernel(c_dec, x_ref, cos_ref, sin_ref, w_in, gn_ref, dec_ref, qd_ref, kd_ref,
                         cw_ref, cb_ref, wg_ref, bg_ref, lam_ref, w_ret, w_lru, ln_g, ln_b,
                         x2_ref, s_ref, h_ref, tail_ref, ext):
    @pl.when(pl.program_id(1) == 0)
    def _():
        s_ref[...] = jnp.zeros_like(s_ref)
        h_ref[...] = jnp.zeros_like(h_ref)
        ext[0:SUBLANES, :] = jnp.zeros((SUBLANES, LRU_WIDTH), F32)

    x = x_ref[...]
    xb = x.astype(BF16)
    lru_in = _dot(xb, w_in[:, LRU_START:])
    qk_raw = _dot(xb, w_in[:, QK_START:VG_START])
    lru_out = _lru_prompt_tile(lru_in[:, :LRU_WIDTH], lru_in[:, LRU_WIDTH:], cw_ref, cb_ref, wg_ref, bg_ref,
                               lam_ref, h_ref, tail_ref, ext)
    vg = _dot(xb, w_in[:, VG_START:LRU_START])
    qk = _rotary_heads(qk_raw, cos_ref[...], sin_ref[...])
    ret_out = _ret_prompt_tile(c_dec, qk[:, :RET_WIDTH], qk[:, RET_WIDTH:], vg[:, :RET_WIDTH], vg[:, RET_WIDTH:],
                               gn_ref, dec_ref, qd_ref, kd_ref, s_ref)
    y = _dot(ret_out, w_ret[...]) + _dot(lru_out.astype(BF16), w_lru[...])
    x2_ref[...] = _layer_norm(DN_ALPHA * x + y, ln_g[...], ln_b[...])


def _mixer_prompt(x, cos, sin, consts, batch, seq):
    decay, q_dec, k_dec, c_dec, _ = _retention_tables()
    w_in, gn_g, *rest = consts
    consts = [w_in, gn_g, decay, q_dec, k_dec] + rest
    consts = [_with_spec(c) for c in consts]
    n_t = seq // SEQ_TILE
    tile = pl.BlockSpec((SEQ_TILE, D_MODEL), lambda b, t: (b * n_t + t, 0))
    table = pl.BlockSpec((SEQ_TILE, HEAD_DIM), lambda b, t: (t, 0))
    small = pl.BlockSpec((1, SUBLANES, LRU_WIDTH), lambda b, t: (b, 0, 0))
    return pl.pallas_call(
        functools.partial(_mixer_prompt_kernel, c_dec),
        grid=(batch, n_t),
        in_specs=[tile, table, table] + [spec for _, spec in consts],
        out_specs=[tile, pl.BlockSpec((1, RET_HEADS, HEAD_DIM, HEAD_DIM), lambda b, t: (b, 0, 0, 0)),
                   small, small],
        out_shape=[jax.ShapeDtypeStruct((batch * seq, D_MODEL), F32),
                   jax.ShapeDtypeStruct((batch, RET_HEADS, HEAD_DIM, HEAD_DIM), F32),
                   jax.ShapeDtypeStruct((batch, SUBLANES, LRU_WIDTH), F32),
                   jax.ShapeDtypeStruct((batch, SUBLANES, LRU_WIDTH), F32)],
        scratch_shapes=[pltpu.VMEM((SEQ_TILE + SUBLANES, LRU_WIDTH), F32)],
        compiler_params=pltpu.CompilerParams(dimension_semantics=("parallel", "arbitrary"),
                                             vmem_limit_bytes=VMEM_LIMIT),
    )(x, cos, sin, *[a for a, _ in consts])


def _lru_sample_kernel(ux_ref, ug_ref, b0_ref, b1_ref, b2_ref, h0_ref, cw_ref, cb_ref, wg_ref, bg_ref,
                       lam_ref, o_ref, h_ref):
    u = ux_ref[...]
    uc = (cb_ref[...] + cw_ref[3:4, :] * u + cw_ref[2:3, :] * b2_ref[...]
          + cw_ref[1:2, :] * b1_ref[...] + cw_ref[0:1, :] * b0_ref[...])
    a, x = _lru_gates(uc, wg_ref[...], bg_ref[...], lam_ref[...])
    h = a * h0_ref[...] + x
    h_ref[...] = h
    o_ref[...] = (h * _gelu_tanh(ug_ref[...])).astype(o_ref.dtype)


def _lru_sample(proj, conv_buf, h0, conv_w, conv_b, w_gates, b_gates, lam):
    n_rows = proj.shape[0]
    ux_col = MIX_ORDER.index("ux")
    col = lambda j: pl.BlockSpec((n_rows, LRU_WIDTH), lambda i: (0, j))
    bufs = [conv_buf[:, j, :] for j in range(CONV_W - 1)]
    rest = [_with_spec(c) for c in bufs + [h0, conv_w, conv_b, w_gates, b_gates, lam]]
    return pl.pallas_call(
        _lru_sample_kernel,
        grid=(1,),
        in_specs=[col(ux_col), col(ux_col + 1)] + [spec for _, spec in rest],
        out_specs=[pl.BlockSpec((n_rows, LRU_WIDTH), lambda i: (0, 0))] * 2,
        out_shape=[jax.ShapeDtypeStruct((n_rows, LRU_WIDTH), BF16),
                   jax.ShapeDtypeStruct((n_rows, LRU_WIDTH), F32)],
        compiler_params=pltpu.CompilerParams(vmem_limit_bytes=VMEM_LIMIT),
    )(proj, proj, *[a for a, _ in rest])


def _pool_finish(x, deltas, w_ref, pb_ref, ps_ref, g_ref, b_ref):
    ys = [_dot(deltas[:, gi * POOL_GROUP:(gi + 1) * POOL_GROUP], w_ref[gi]) for gi in range(len(POOL_WINDOWS))]
    y = (jnp.concatenate(ys, axis=1) + pb_ref[...]) * ps_ref[...]
    return _layer_norm(DN_ALPHA * x + y, g_ref[...], b_ref[...])


POOL_HALO = POOL_BUF + 1


POOL_ROWS = 64


def _pool_deltas(tiles_per_seq, tile, x_ref, halo, out, groups):
    n = ROW_TILE
    t = tile % tiles_per_seq
    for gi in groups:
        wnd = POOL_WINDOWS[gi]
        cols = slice(gi * POOL_GROUP, (gi + 1) * POOL_GROUP)
        for r0 in range(0, n, POOL_ROWS):
            rows = slice(r0, r0 + POOL_ROWS)
            if r0 == 0:
                before = jnp.where(t == 0, 0.0, halo[:, cols])
                w = jnp.concatenate([before, x_ref[rows, cols]], axis=0)
            else:
                w = x_ref[r0 - POOL_HALO:r0 + POOL_ROWS, cols]
            shift = 1
            while shift < wnd:
                w = w + pltpu.roll(w, shift, 0)
                shift *= 2
            pos = t * n + r0 + lax.broadcasted_iota(jnp.int32, (POOL_ROWS, POOL_GROUP), 0)
            cnt = jnp.minimum(wnd, pos + 1).astype(F32)
            out[rows, cols] = (w[POOL_HALO:, :] / cnt - x_ref[rows, cols]).astype(BF16)
        halo[:, cols] = x_ref[n - POOL_HALO:, cols]


def _pool_ln_prompt(tiles_per_seq, tile, rows, consts, scratch, parity, first):
    x_ref, x_next_ref = rows
    halo, stash_even, stash_odd = scratch
    stash = (stash_even, stash_odd)
    all_groups = range(len(POOL_WINDOWS))
    if first:
        halo[...] = jnp.zeros_like(halo)
        _pool_deltas(tiles_per_seq, tile, x_ref, halo, stash[parity], all_groups)
    x2 = _pool_finish(x_ref[...], stash[parity][...], *consts)
    ahead = functools.partial(_pool_deltas, tiles_per_seq, tile + 1, x_next_ref, halo, stash[1 - parity])
    return x2, (lambda: ahead(all_groups[2:]), lambda: ahead(all_groups[:2]))


def _pool_sample_kernel(x_ref, buf_ref, w_ref, pb_ref, ps_ref, g_ref, b_ref, o_ref):
    x = x_ref[...]
    deltas = []
    for gi, wnd in enumerate(POOL_WINDOWS):
        cols = slice(gi * POOL_GROUP, (gi + 1) * POOL_GROUP)
        s = x[:, cols]
        for j in range(1, wnd):
            s = s + buf_ref[POOL_BUF - j, :, cols]
        cnt = min(float(wnd), PAST_LEN + 1.0)
        deltas.append((s / cnt - x[:, cols]).astype(BF16))
    o_ref[...] = _pool_finish(x, jnp.concatenate(deltas, axis=1), w_ref, pb_ref, ps_ref, g_ref, b_ref)


def _pool_sample(x, buf_t, consts):
    full = lambda a: pl.BlockSpec(a.shape, lambda i: (0,) * a.ndim)
    consts = [_with_spec(c) for c in consts]
    return pl.pallas_call(
        _pool_sample_kernel,
        grid=(1,),
        in_specs=[full(x), full(buf_t)] + [spec for _, spec in consts],
        out_specs=full(x),
        out_shape=jax.ShapeDtypeStruct(x.shape, F32),
        compiler_params=pltpu.CompilerParams(vmem_limit_bytes=VMEM_LIMIT),
    )(x, buf_t, *[a for a, _ in consts])


def _rotary_tables(positions):
    half = HEAD_DIM // 2
    inv = ROPE_BASE ** (-jnp.arange(half, dtype=F32) / half)
    ang = positions.astype(F32)[:, None] * inv[None, :]
    cos, sin = jnp.cos(ang), jnp.sin(ang)
    return jnp.concatenate([cos, cos], axis=1), jnp.concatenate([-sin, sin], axis=1)


def _block_diag(w):
    layers, nb, n, _ = w.shape
    eye = jnp.eye(nb, dtype=w.dtype)
    return (eye[None, :, None, :, None] * w[:, :, :, None, :]).reshape(layers, nb * n, nb * n)


def kernel(x_prompt, x_sample, state_ret, state_lru_h, state_lru_conv, state_pool, w_ffn_gate, w_ffn_up, w_ffn_down, ln_g, ln_b, w_mix_in, w_mix_out, ret_gn_g, lru_conv_w, lru_conv_b, lru_wa, lru_ba, lru_wi, lru_bi, lru_lambda, pool_w, pool_b, pool_scale):
    batch, seq, _ = x_prompt.shape
    n_s = x_sample.shape[0]
    assert x_sample.shape[1] == 1 and seq % SEQ_TILE == 0 and n_s % SAMPLE_GROUP == 0
    xp = x_prompt.reshape(batch * seq, D_MODEL)
    xs = x_sample.reshape(n_s, D_MODEL)
    rows_of = lambda a: a.reshape(a.shape[:-1] + (1, a.shape[-1]))

    cos_p, sin_p = _rotary_tables(jnp.arange(seq))
    cos_s, sin_s = _rotary_tables(jnp.full((n_s,), PAST_LEN))

    w_in_all, w_out_all, pool_w_all = (w.astype(BF16) for w in (w_mix_in, w_mix_out, pool_w))
    ln_g_rows, ln_b_rows = rows_of(ln_g), rows_of(ln_b)
    ln = lambda layer, i: [_pick(ln_g_rows, layer, i), _pick(ln_b_rows, layer, i)]
    gn_rows, conv_b_rows, lam_rows = rows_of(ret_gn_g), rows_of(lru_conv_b), rows_of(lru_lambda)
    pool_b_rows, pool_scale_rows = rows_of(pool_b), rows_of(pool_scale)
    w_gates_all = jnp.concatenate([_block_diag(lru_wa), _block_diag(lru_wi)], axis=-1).astype(BF16)
    b_gates_rows = rows_of(jnp.concatenate([lru_ba, lru_bi], axis=-1))
    ffn_stacks = (w_ffn_gate, w_ffn_up, w_ffn_down)
    ffn_w = [w[0, 0].astype(BF16) for w in ffn_stacks]

    def ffn(pre_p, pre_s, rows_p, rows_s, pre_consts, layer, k, **kw):
        nonlocal ffn_w
        consts = ffn_w + ln(layer, 2 * k)
        nxt = (layer, 1) if k == 0 else (layer + 1, 0)
        cast_next = None if nxt[0] == DEPTH else (*ffn_stacks, *nxt)
        xp, xs, *cast = _ffn_pipeline(pre_p, pre_s, rows_p, rows_s, pre_consts, consts,
                                      cast_next=cast_next, **kw)
        ffn_w = cast
        return xp, xs

    rets_p, hs_p, convs_p, pools_p = [], [], [], []
    hs_s, convs_s, pools_s = [], [], []
    projs_s = []
    for layer in range(DEPTH):
        j = layer // 2
        xp, xs = ffn(_same(_identity_pre), _identity_pre, [xp], [xs], [], layer, 0)
        if layer % 2 == 0:
            gn = _pick(gn_rows, j)
            lru_consts = (_pick(lru_conv_w, j), _pick(conv_b_rows, j), _pick(w_gates_all, j),
                          _pick(b_gates_rows, j), _pick(lam_rows, j))
            out_consts = [_pick(w_out_all, j, rows=(RET_WIDTH, 0)), _pick(w_out_all, j, rows=(LRU_WIDTH, 1)),
                          *ln(layer, 1)]

            x1_s = xs
            xp, s_p, h_p, tail_p = _mixer_prompt(
                xp, cos_p, sin_p, [_pick(w_in_all, j), gn, *lru_consts, *out_consts], batch, seq)
            proj_s = _mix_in_sample(x1_s, cos_s, sin_s, _pick(w_in_all, j))
            projs_s.append(proj_s)
            ret_s, *rets_s = _ret_sample(projs_s, gn, state_ret, write_states=len(projs_s) == state_ret.shape[0])
            lru_s, h_s = _lru_sample(proj_s, state_lru_conv[j], _pick(state_lru_h, j), *lru_consts)

            rets_p.append(s_p)
            hs_p.append(h_p[:, 0, :])
            convs_p.append(tail_p[:, SUBLANES - (CONV_W - 1):, :])
            hs_s.append(h_s)
            ux_s = proj_s[:, MIX_ORDER.index("ux") * LRU_WIDTH:][:, :LRU_WIDTH]
            convs_s.append(jnp.concatenate([state_lru_conv[j][:, 1:, :], ux_s[:, None, :]], axis=1))

            xp, xs = ffn(_same(_identity_pre), _mix_out_ln, [xp], [x1_s, ret_s, lru_s], out_consts, layer, 1)
        else:
            pool_consts = [_pick(pool_w_all, j), _pick(pool_b_rows, j), _pick(pool_scale_rows, j), *ln(layer, 1)]
            pools_p.append(xp.reshape(batch, seq, D_MODEL)[:, seq - POOL_BUF:, :])
            pools_s.append(jnp.concatenate([state_pool[j][:, 1:, :], xs[:, None, :]], axis=1))
            xs = _pool_sample(xs, jnp.transpose(state_pool[j], (1, 0, 2)), pool_consts)
            xp, xs = ffn(functools.partial(_pool_ln_prompt, seq // ROW_TILE), lambda rows, consts: rows[0],
                         [xp, xp], [xs], pool_consts, layer, 1, row_leads=[0, 1],
                         scratch=[pltpu.VMEM((POOL_HALO, D_MODEL), F32)]
                         + [pltpu.VMEM((ROW_TILE, D_MODEL), BF16)] * 2)

    return (xp.reshape(batch, seq, D_MODEL), xs.reshape(n_s, 1, D_MODEL),
            jnp.stack(rets_p), jnp.stack(hs_p), jnp.stack(convs_p), jnp.stack(pools_p),
            rets_s[0], jnp.stack(hs_s), jnp.stack(convs_s), jnp.stack(pools_s))
```

```python
import functools
import math

import numpy as np
import jax
import jax.numpy as jnp
from jax import lax
from jax.experimental import pallas as pl
from jax.experimental.pallas import tpu as pltpu

F32 = jnp.float32
BF16 = jnp.bfloat16

D_MODEL = 1024
DEPTH = 4
PAST_LEN = 16384
RET_HEADS = 4
HEAD_DIM = 128
RET_WIDTH = RET_HEADS * HEAD_DIM
RET_CHUNK = 128
ROPE_BASE = 10000.0
LRU_WIDTH = 512
LRU_C = 8.0
CONV_W = 4
POOL_WINDOWS = (2, 4, 8, 16)
POOL_GROUP = D_MODEL // len(POOL_WINDOWS)
POOL_BUF = max(POOL_WINDOWS) - 1
LN_EPS = 1e-5
DN_ALPHA = (2.0 * DEPTH) ** 0.25

SUBLANES = 8
BF16_SUBLANES = 16
ROW_TILE = 512
SEQ_TILE = 1024
SAMPLE_GROUP = 16
CAST_BLOCKS = 32
VMEM_LIMIT = 56 * 1024 * 1024


def _sigmoid(x):
    return 0.5 * jnp.tanh(0.5 * x) + 0.5


def _silu(x):
    return x * _sigmoid(x)


def _gelu_tanh(x):
    return x * (0.5 * (1.0 + jnp.tanh(math.sqrt(2.0 / math.pi) * (x + 0.044715 * (x * x * x)))))


def _layer_norm(z, g, b):
    mu = jnp.mean(z, axis=-1, keepdims=True)
    d = z - mu
    var = jnp.mean(d * d, axis=-1, keepdims=True)
    return d * lax.rsqrt(var + LN_EPS) * g + b


def _unit_norm(z):
    mu = jnp.mean(z, axis=-1, keepdims=True)
    d = z - mu
    var = jnp.mean(d * d, axis=-1, keepdims=True)
    return d * lax.rsqrt(var + LN_EPS)


def _dot(a, b):
    return jnp.dot(a, b, preferred_element_type=F32)


def _resident(arr):
    nd = arr.ndim
    return pl.BlockSpec(arr.shape, lambda *_: (0,) * nd, pipeline_mode=pl.Buffered(1))


def _with_spec(const):
    return const if isinstance(const, tuple) else (const, _resident(const))


def _pick(arr, *lead, rows=None):
    tail = arr.shape[len(lead):]
    first = 0
    if rows is not None:
        tail = (rows[0],) + tail[1:]
        first = rows[1]
    index = tuple(lead) + (first,) + (0,) * (len(tail) - 1)
    return arr, pl.BlockSpec((None,) * len(lead) + tail, lambda *_: index, pipeline_mode=pl.Buffered(1))


def _same(body):
    return lambda tile, rows, consts, scratch, parity, first: (body([r[...] for r in rows], consts), ())


def _ffn_residual(x, wg, wu, wd, between=()):
    between = list(between) + [None, None]
    xb = x.astype(BF16)
    hg = _dot(xb, wg[...])
    if between[0] is not None:
        between[0]()
    hu = _dot(xb, wu[...])
    act = (_silu(hg) * hu).astype(BF16)
    if between[1] is not None:
        between[1]()
    return DN_ALPHA * x + 0.5 * _dot(act, wd[...])


def _ffn_pipeline(pre_p, pre_s, rows_p, rows_s, pre_consts, ffn_consts, scratch=(), cast_next=None,
                  row_leads=None):
    n_rows_p = rows_p[0].shape[0]
    n_rows_s = rows_s[0].shape[0]
    assert n_rows_p % ROW_TILE == 0
    n_p = n_rows_p // ROW_TILE
    n_in_p, n_in_s, n_pc = len(rows_p), len(rows_s), len(pre_consts)
    consts = [_with_spec(c) for c in list(pre_consts) + list(ffn_consts)]
    n_cast = 0 if cast_next is None else 3

    def kern(*refs):
        refs = list(refs)
        p_refs, refs = refs[:n_in_p], refs[n_in_p:]
        s_refs, refs = refs[:n_in_s], refs[n_in_s:]
        pc_refs, refs = refs[:n_pc], refs[n_pc:]
        (wg, wu, wd, ln_g, ln_b), refs = refs[:5], refs[5:]
        cast_in, refs = refs[:n_cast], refs[n_cast:]
        (out_p, out_s), refs = refs[:2], refs[2:]
        cast_out, refs = refs[:n_cast], refs[n_cast:]
        (x2_even, x2_odd, z_scr), scr = refs[:3], refs[3:]
        g = pl.program_id(0)
        steady = (g >= 1) & (g <= n_p)


        @pl.when(g == 0)
        def _():
            z_scr[...] = jnp.zeros_like(z_scr)
            x2, later = pre_p(g, p_refs, pc_refs, scr, 0, True)
            x2_even[...] = x2
            for work in later:
                work()

        def steady_step(x2_cur, x2_next, parity):
            later = []

            def after_gate():
                x2, work = pre_p(g, p_refs, pc_refs, scr, parity, False)
                x2_next[...] = x2
                later.extend(work)
                if later:
                    later.pop(0)()

            def before_down():
                if later:
                    later.pop(0)()
                out_p[...] = _layer_norm(z_scr[...], ln_g[...], ln_b[...])
                for src, dst in zip(cast_in, cast_out):
                    dst[...] = src[...].astype(dst.dtype)

            z_scr[...] = _ffn_residual(x2_cur[...], wg, wu, wd, (after_gate, before_down))

        pl.when(steady & (g % 2 == 1))(lambda: steady_step(x2_even, x2_odd, 1))
        pl.when(steady & (g % 2 == 0))(lambda: steady_step(x2_odd, x2_even, 0))

        @pl.when(g == n_p + 1)
        def _():
            out_p[...] = _layer_norm(z_scr[...], ln_g[...], ln_b[...])
            x2 = pre_s([r[...] for r in s_refs], pc_refs)
            out_s[...] = _layer_norm(_ffn_residual(x2, wg, wu, wd), ln_g[...], ln_b[...])

    tile = lambda cols, lag: pl.BlockSpec(
        (ROW_TILE, cols), lambda g: (jnp.clip(g - lag, 0, n_p - 1), 0))
    in_specs = ([tile(a.shape[1], -lead) for a, lead in zip(rows_p, row_leads or [0] * n_in_p)]
                + [pl.BlockSpec(a.shape, lambda g: (0, 0)) for a in rows_s]
                + [spec for _, spec in consts])
    out_specs = [tile(D_MODEL, 2), pl.BlockSpec((n_rows_s, D_MODEL), lambda g: (0, 0))]
    out_shape = [jax.ShapeDtypeStruct((n_rows_p, D_MODEL), F32), jax.ShapeDtypeStruct((n_rows_s, D_MODEL), F32)]
    cast_operands = []
    if cast_next is not None:
        *stacked, layer, k = cast_next
        for w in stacked:
            rows, cols = w.shape[2:]
            hold = max(1, n_p // CAST_BLOCKS)
            size = next(s for s in range(BF16_SUBLANES, rows + 1, BF16_SUBLANES)
                        if rows % s == 0 and rows // s * hold <= n_p)
            block = lambda g, last=rows // size - 1, hold=hold: jnp.clip((g - 1) // hold, 0, last)
            in_specs.append(pl.BlockSpec((None, None, size, cols), lambda g, b=block: (layer, k, b(g), 0)))
            out_specs.append(pl.BlockSpec((size, cols), lambda g, b=block: (b(g), 0)))
            out_shape.append(jax.ShapeDtypeStruct((rows, cols), BF16))
            cast_operands.append(w)
    return pl.pallas_call(
        kern, grid=(n_p + 2,), in_specs=in_specs, out_specs=out_specs, out_shape=out_shape,
        scratch_shapes=[pltpu.VMEM((ROW_TILE, D_MODEL), F32)] * 3 + list(scratch),
        compiler_params=pltpu.CompilerParams(dimension_semantics=("arbitrary",),
                                             vmem_limit_bytes=VMEM_LIMIT),
    )(*rows_p, *rows_s, *[a for a, _ in consts], *cast_operands)


def _identity_pre(rows, consts):
    (x,) = rows
    return x


def _mix_in_sample_kernel(x_ref, cos_ref, sin_ref, w_ref, o_ref):
    o_ref[...] = _rotary_qk(_dot(x_ref[...].astype(BF16), w_ref[...]), cos_ref[...], sin_ref[...])


def _mix_in_sample(x, cos, sin, w_in):
    full = lambda a: pl.BlockSpec(a.shape, lambda i: (0,) * a.ndim)
    out = jax.ShapeDtypeStruct((x.shape[0], MIX_COLS), F32)
    return pl.pallas_call(
        _mix_in_sample_kernel, grid=(1,), in_specs=[full(x), full(cos), full(sin), w_in[1]],
        out_specs=full(out), out_shape=out,
        compiler_params=pltpu.CompilerParams(vmem_limit_bytes=VMEM_LIMIT),
    )(x, cos, sin, w_in[0])


def _mix_out_ln(rows, consts):
    x, ret, lru = rows
    w_ret, w_lru, g, b = consts
    y = _dot(ret, w_ret[...]) + _dot(lru, w_lru[...])
    return _layer_norm(DN_ALPHA * x + y, g[...], b[...])


def _retention_tables():
    lg = np.log1p(-np.exp2(-5.0 - np.arange(RET_HEADS, dtype=np.float64)))
    idx = np.arange(RET_CHUNK, dtype=np.float64)
    diff = idx[:, None] - idx[None, :]
    decay = np.where(diff >= 0, np.exp(lg[:, None, None] * np.maximum(diff, 0.0)), 0.0)
    ones = np.ones((1, 1, HEAD_DIM))
    q_dec = np.exp(lg[:, None] * (idx + 1.0))[..., None] * ones
    k_dec = np.exp(lg[:, None] * (RET_CHUNK - 1.0 - idx))[..., None] * ones
    c_dec = np.exp(lg * RET_CHUNK)
    as32 = lambda a: jnp.asarray(a.astype(np.float32))
    return as32(decay), as32(q_dec), as32(k_dec), [float(c) for c in c_dec], [float(g) for g in np.exp(lg)]


def _ret_prompt_tile(c_dec, q_ref, k_ref, v_ref, g_ref, gn_ref, dec_ref, qd_ref, kd_ref, states):
    heads = range(RET_HEADS)
    head_cols = [slice(h * HEAD_DIM, (h + 1) * HEAD_DIM) for h in heads]
    states = list(states)
    out_rows = []
    for c in range(SEQ_TILE // RET_CHUNK):
        rows = slice(c * RET_CHUNK, (c + 1) * RET_CHUNK)
        out_heads = []
        q = [q_ref[rows, cols] for cols in head_cols]
        k = [k_ref[rows, cols] for cols in head_cols]
        vb = [v_ref[rows, cols].astype(BF16) for cols in head_cols]
        scores = [lax.dot_general(q[h].astype(BF16), k[h].astype(BF16), (((1,), (1,)), ((), ())),
                                  preferred_element_type=F32) for h in heads]
        update = [_dot((k[h] * kd_ref[h]).T.astype(BF16), vb[h]) for h in heads]
        for h in heads:
            lhs = jnp.concatenate([(scores[h] * dec_ref[h]).astype(BF16), (q[h] * qd_ref[h]).astype(BF16)], axis=1)
            o = _dot(lhs, jnp.concatenate([vb[h], states[h].astype(BF16)], axis=0))
            states[h] = states[h] * c_dec[h] + update[h]
            o = _unit_norm(o) * gn_ref[:, head_cols[h]]
            out_heads.append((_silu(g_ref[rows, head_cols[h]]) * o).astype(BF16))
        out_rows.append(jnp.concatenate(out_heads, axis=1))
    return jnp.concatenate(out_rows, axis=0), states


def _ret_sample_kernel(gamma, q_ref, k_ref, v_ref, g_ref, gn_ref, s_in, *rest):
    earlier, outs = rest[:len(rest) // 3 * 3], rest[len(rest) // 3 * 3:]
    o_ref = outs[0]
    s_out = outs[1] if len(outs) > 1 else None
    n = SAMPLE_GROUP
    row = lax.broadcasted_iota(jnp.int32, (n, n * HEAD_DIM), 0)
    blk = lax.broadcasted_iota(jnp.int32, (n, n * HEAD_DIM), 1) // HEAD_DIM
    own = row == blk

    def new_states(k_ref, v_ref, s_ref, h):
        cols = slice(h * HEAD_DIM, (h + 1) * HEAD_DIM)
        v_wide = jnp.where(own, jnp.concatenate([v_ref[:, cols]] * n, axis=1), 0.0).astype(BF16)
        outer = _dot(k_ref[:, cols].T.astype(BF16), v_wide)
        return [s_ref[r, h] * gamma[h] + outer[:, r * HEAD_DIM:(r + 1) * HEAD_DIM] for r in range(n)]

    for h in range(RET_HEADS):
        cols = slice(h * HEAD_DIM, (h + 1) * HEAD_DIM)
        for slot in range(len(earlier) // 3):
            for r, s_new in enumerate(new_states(*earlier[3 * slot:3 * slot + 3], h)):
                s_out[slot, r, h] = s_new
        new = new_states(k_ref, v_ref, s_in, h)
        if s_out is not None:
            for r, s_new in enumerate(new):
                s_out[len(earlier) // 3, r, h] = s_new
        q = q_ref[:, cols]
        wide = _dot(q.astype(BF16), jnp.concatenate([s.astype(BF16) for s in new], axis=1))
        wide = jnp.where(own, wide, 0.0)
        o = wide[:, :HEAD_DIM]
        for r in range(1, n):
            o = o + wide[:, r * HEAD_DIM:(r + 1) * HEAD_DIM]
        o = _unit_norm(o) * gn_ref[:, cols]
        o_ref[:, cols] = (_silu(g_ref[:, cols]) * o).astype(o_ref.dtype)


def _ret_sample(projs, gn_g, states, write_states):
    _, _, _, _, gamma = _retention_tables()
    j = len(projs) - 1
    n_rows = projs[-1].shape[0]
    n = SAMPLE_GROUP
    col = lambda name: pl.BlockSpec((n, RET_WIDTH), lambda i: (i, MIX_ORDER.index(name)))
    state = lambda slot: pl.BlockSpec((None, n, RET_HEADS, HEAD_DIM, HEAD_DIM), lambda i: (slot, i, 0, 0, 0))
    gn_g, gn_spec = _with_spec(gn_g)
    operands = [projs[-1]] * 4 + [gn_g, states]
    in_specs = [col("q"), col("k"), col("v"), col("g"), gn_spec, state(j)]
    out_specs = [pl.BlockSpec((n, RET_WIDTH), lambda i: (i, 0))]
    out_shape = [jax.ShapeDtypeStruct((n_rows, RET_WIDTH), BF16)]
    if write_states:
        for slot, proj in enumerate(projs[:-1]):
            operands += [proj, proj, states]
            in_specs += [col("k"), col("v"), state(slot)]
        out_specs.append(pl.BlockSpec((j + 1, n, RET_HEADS, HEAD_DIM, HEAD_DIM), lambda i: (0, i, 0, 0, 0)))
        out_shape.append(jax.ShapeDtypeStruct((j + 1,) + states.shape[1:], F32))
    return pl.pallas_call(
        functools.partial(_ret_sample_kernel, gamma),
        grid=(n_rows // n,),
        in_specs=in_specs, out_specs=out_specs, out_shape=out_shape,
        compiler_params=pltpu.CompilerParams(dimension_semantics=("parallel",),
                                             vmem_limit_bytes=VMEM_LIMIT),
    )(*operands)


def _lru_gates(uc, w_gates, b_gates, lam):
    pre = _dot(uc.astype(BF16), w_gates) + b_gates
    r = _sigmoid(pre[:, :LRU_WIDTH])
    i = _sigmoid(pre[:, LRU_WIDTH:])
    neg = -lam
    softplus = jnp.maximum(neg, 0.0) + jnp.log1p(jnp.exp(-jnp.abs(neg)))
    log_a = (-LRU_C * softplus) * r
    a = jnp.exp(log_a)
    xin = jnp.sqrt(jnp.tanh(-log_a) * (1.0 + a * a)) * (i * uc)
    return a, xin


def _lru_prompt_tile(u, ug, cw_ref, cb_ref, wg_ref, bg_ref, lam_ref, h, before, ext):
    n = SEQ_TILE
    pad = SUBLANES
    ext[0:pad, :] = before
    ext[pad:pad + n, :] = u
    uc = cb_ref[...] + cw_ref[CONV_W - 1:CONV_W, :] * u
    for j in range(1, CONV_W):
        uc = uc + cw_ref[CONV_W - 1 - j:CONV_W - j, :] * ext[pad - j:pad - j + n, :]

    a, x = _lru_gates(uc, wg_ref[...], bg_ref[...], lam_ref[...])

    groups = (n // SUBLANES, SUBLANES, LRU_WIDTH)
    a = a.reshape(groups)
    x = x.reshape(groups)
    step = lax.broadcasted_iota(jnp.int32, groups, 1)
    shift = 1
    while shift < SUBLANES:
        inside = step >= shift
        x = a * jnp.where(inside, pltpu.roll(x, shift, 1), 0.0) + x
        a = a * jnp.where(inside, pltpu.roll(a, shift, 1), 1.0)
        shift *= 2
    hs = []
    for i in range(n // SUBLANES):
        hg = a[i] * h + x[i]
        hs.append(hg)
        h = jnp.broadcast_to(hg[SUBLANES - 1:SUBLANES, :], (SUBLANES, LRU_WIDTH))
    return jnp.concatenate(hs, axis=0) * _gelu_tanh(ug), h, u[n - pad:, :]


QKVG_COLS = 2 * RET_WIDTH * 2
MIX_COLS = QKVG_COLS + 2 * LRU_WIDTH
MIX_ORDER = ("q", "k", "v", "g", "ux", "ug")
QK_START = MIX_ORDER.index("q") * RET_WIDTH
VG_START = MIX_ORDER.index("v") * RET_WIDTH
LRU_START = MIX_ORDER.index("ux") * RET_WIDTH


def _rotary_heads(qk, cos, sin):
    pieces = []
    for j in range(2 * RET_HEADS):
        blk = qk[:, j * HEAD_DIM:(j + 1) * HEAD_DIM]
        rot = blk * cos + pltpu.roll(blk, HEAD_DIM // 2, 1) * sin
        if j < RET_HEADS:
            rot = rot * (HEAD_DIM ** -0.5)
        pieces.append(rot)
    return jnp.concatenate(pieces, axis=1)


def _rotary_qk(proj, cos, sin):
    assert QK_START == 0
    return jnp.concatenate([_rotary_heads(proj[:, :VG_START], cos, sin), proj[:, VG_START:]], axis=1)


def _mixer_prompt_kernel(c_dec, x_ref, cos_ref, sin_ref, w_in, gn_ref, dec_ref, qd_ref, kd_ref,
                         cw_ref, cb_ref, wg_ref, bg_ref, lam_ref, w_ret, w_lru, ln_g, ln_b,
                         x2_ref, s_ref, h_ref, tail_ref, ext):
    heads = range(RET_HEADS)

    @pl.when(pl.program_id(1) == 0)
    def _():
        s_ref[...] = jnp.zeros_like(s_ref)
        h_ref[...] = jnp.zeros_like(h_ref)
        tail_ref[...] = jnp.zeros_like(tail_ref)

    x = x_ref[...]
    xb = x.astype(BF16)
    lru_in = _dot(xb, w_in[:, LRU_START:])
    qk_raw = _dot(xb, w_in[:, QK_START:VG_START])
    lru_out, h_new, tail = _lru_prompt_tile(lru_in[:, :LRU_WIDTH], lru_in[:, LRU_WIDTH:], cw_ref, cb_ref, wg_ref,
                                            bg_ref, lam_ref, h_ref[0], tail_ref[0], ext)
    vg = _dot(xb, w_in[:, VG_START:LRU_START])
    qk = _rotary_heads(qk_raw, cos_ref[...], sin_ref[...])
    ret_out, states = _ret_prompt_tile(
        c_dec, qk[:, :RET_WIDTH], qk[:, RET_WIDTH:], vg[:, :RET_WIDTH], vg[:, RET_WIDTH:],
        gn_ref, dec_ref, qd_ref, kd_ref, [s_ref[0, h] for h in heads])
    h_ref[0] = h_new
    tail_ref[0] = tail
    for h in heads:
        s_ref[0, h] = states[h]
    y = _dot(ret_out, w_ret[...]) + _dot(lru_out.astype(BF16), w_lru[...])
    x2_ref[...] = _layer_norm(DN_ALPHA * x + y, ln_g[...], ln_b[...])


def _mixer_prompt(x, cos, sin, consts, batch, seq):
    decay, q_dec, k_dec, c_dec, _ = _retention_tables()
    w_in, gn_g, *rest = consts
    consts = [w_in, gn_g, decay, q_dec, k_dec] + rest
    consts = [_with_spec(c) for c in consts]
    n_t = seq // SEQ_TILE
    tile = pl.BlockSpec((SEQ_TILE, D_MODEL), lambda b, t: (b * n_t + t, 0))
    table = pl.BlockSpec((SEQ_TILE, HEAD_DIM), lambda b, t: (t, 0))
    small = pl.BlockSpec((1, SUBLANES, LRU_WIDTH), lambda b, t: (b, 0, 0))
    return pl.pallas_call(
        functools.partial(_mixer_prompt_kernel, c_dec),
        grid=(batch, n_t),
        in_specs=[tile, table, table] + [spec for _, spec in consts],
        out_specs=[tile, pl.BlockSpec((1, RET_HEADS, HEAD_DIM, HEAD_DIM), lambda b, t: (b, 0, 0, 0)),
                   small, small],
        out_shape=[jax.ShapeDtypeStruct((batch * seq, D_MODEL), F32),
                   jax.ShapeDtypeStruct((batch, RET_HEADS, HEAD_DIM, HEAD_DIM), F32),
                   jax.ShapeDtypeStruct((batch, SUBLANES, LRU_WIDTH), F32),
                   jax.ShapeDtypeStruct((batch, SUBLANES, LRU_WIDTH), F32)],
        scratch_shapes=[pltpu.VMEM((SEQ_TILE + SUBLANES, LRU_WIDTH), F32)],
        compiler_params=pltpu.CompilerParams(dimension_semantics=("parallel", "arbitrary"),
                                             vmem_limit_bytes=VMEM_LIMIT),
    )(x, cos, sin, *[a for a, _ in consts])


def _lru_sample_kernel(ux_ref, ug_ref, b0_ref, b1_ref, b2_ref, h0_ref, cw_ref, cb_ref, wg_ref, bg_ref,
                       lam_ref, o_ref, h_ref):
    u = ux_ref[...]
    uc = (cb_ref[...] + cw_ref[3:4, :] * u + cw_ref[2:3, :] * b2_ref[...]
          + cw_ref[1:2, :] * b1_ref[...] + cw_ref[0:1, :] * b0_ref[...])
    a, x = _lru_gates(uc, wg_ref[...], bg_ref[...], lam_ref[...])
    h = a * h0_ref[...] + x
    h_ref[...] = h
    o_ref[...] = (h * _gelu_tanh(ug_ref[...])).astype(o_ref.dtype)


def _lru_sample(proj, conv_buf, h0, conv_w, conv_b, w_gates, b_gates, lam):
    n_rows = proj.shape[0]
    ux_col = MIX_ORDER.index("ux")
    col = lambda j: pl.BlockSpec((n_rows, LRU_WIDTH), lambda i: (0, j))
    bufs = [conv_buf[:, j, :] for j in range(CONV_W - 1)]
    rest = [_with_spec(c) for c in bufs + [h0, conv_w, conv_b, w_gates, b_gates, lam]]
    return pl.pallas_call(
        _lru_sample_kernel,
        grid=(1,),
        in_specs=[col(ux_col), col(ux_col + 1)] + [spec for _, spec in rest],
        out_specs=[pl.BlockSpec((n_rows, LRU_WIDTH), lambda i: (0, 0))] * 2,
        out_shape=[jax.ShapeDtypeStruct((n_rows, LRU_WIDTH), BF16),
                   jax.ShapeDtypeStruct((n_rows, LRU_WIDTH), F32)],
        compiler_params=pltpu.CompilerParams(vmem_limit_bytes=VMEM_LIMIT),
    )(proj, proj, *[a for a, _ in rest])


def _pool_finish(x, deltas, w_ref, pb_ref, ps_ref, g_ref, b_ref):
    ys = [_dot(deltas[:, gi * POOL_GROUP:(gi + 1) * POOL_GROUP], w_ref[gi]) for gi in range(len(POOL_WINDOWS))]
    y = (jnp.concatenate(ys, axis=1) + pb_ref[...]) * ps_ref[...]
    return _layer_norm(DN_ALPHA * x + y, g_ref[...], b_ref[...])


POOL_HALO = POOL_BUF + 1


POOL_ROWS = 64


def _pool_deltas(tiles_per_seq, tile, x_ref, halo, out, groups):
    n = ROW_TILE
    t = tile % tiles_per_seq
    for gi in groups:
        wnd = POOL_WINDOWS[gi]
        cols = slice(gi * POOL_GROUP, (gi + 1) * POOL_GROUP)
        for r0 in range(0, n, POOL_ROWS):
            rows = slice(r0, r0 + POOL_ROWS)
            if r0 == 0:
                before = jnp.where(t == 0, 0.0, halo[:, cols])
                w = jnp.concatenate([before, x_ref[rows, cols]], axis=0)
            else:
                w = x_ref[r0 - POOL_HALO:r0 + POOL_ROWS, cols]
            shift = 1
            while shift < wnd:
                w = w + pltpu.roll(w, shift, 0)
                shift *= 2
            pos = t * n + r0 + lax.broadcasted_iota(jnp.int32, (POOL_ROWS, POOL_GROUP), 0)
            cnt = jnp.minimum(wnd, pos + 1).astype(F32)
            out[rows, cols] = (w[POOL_HALO:, :] / cnt - x_ref[rows, cols]).astype(BF16)
        halo[:, cols] = x_ref[n - POOL_HALO:, cols]


def _pool_ln_prompt(tiles_per_seq, tile, rows, consts, scratch, parity, first):
    x_ref, x_next_ref = rows
    halo, stash_even, stash_odd = scratch
    stash = (stash_even, stash_odd)
    all_groups = range(len(POOL_WINDOWS))
    if first:
        halo[...] = jnp.zeros_like(halo)
        _pool_deltas(tiles_per_seq, tile, x_ref, halo, stash[parity], all_groups)
    x2 = _pool_finish(x_ref[...], stash[parity][...], *consts)
    ahead = functools.partial(_pool_deltas, tiles_per_seq, tile + 1, x_next_ref, halo, stash[1 - parity])
    return x2, (lambda: ahead(all_groups[2:]), lambda: ahead(all_groups[:2]))


def _pool_sample_kernel(x_ref, buf_ref, w_ref, pb_ref, ps_ref, g_ref, b_ref, o_ref):
    x = x_ref[...]
    deltas = []
    for gi, wnd in enumerate(POOL_WINDOWS):
        cols = slice(gi * POOL_GROUP, (gi + 1) * POOL_GROUP)
        s = x[:, cols]
        for j in range(1, wnd):
            s = s + buf_ref[POOL_BUF - j, :, cols]
        cnt = min(float(wnd), PAST_LEN + 1.0)
        deltas.append((s / cnt - x[:, cols]).astype(BF16))
    o_ref[...] = _pool_finish(x, jnp.concatenate(deltas, axis=1), w_ref, pb_ref, ps_ref, g_ref, b_ref)


def _pool_sample(x, buf_t, consts):
    full = lambda a: pl.BlockSpec(a.shape, lambda i: (0,) * a.ndim)
    consts = [_with_spec(c) for c in consts]
    return pl.pallas_call(
        _pool_sample_kernel,
        grid=(1,),
        in_specs=[full(x), full(buf_t)] + [spec for _, spec in consts],
        out_specs=full(x),
        out_shape=jax.ShapeDtypeStruct(x.shape, F32),
        compiler_params=pltpu.CompilerParams(vmem_limit_bytes=VMEM_LIMIT),
    )(x, buf_t, *[a for a, _ in consts])


def _rotary_tables(positions):
    half = HEAD_DIM // 2
    inv = ROPE_BASE ** (-jnp.arange(half, dtype=F32) / half)
    ang = positions.astype(F32)[:, None] * inv[None, :]
    cos, sin = jnp.cos(ang), jnp.sin(ang)
    return jnp.concatenate([cos, cos], axis=1), jnp.concatenate([-sin, sin], axis=1)


def _block_diag(w):
    layers, nb, n, _ = w.shape
    eye = jnp.eye(nb, dtype=w.dtype)
    return (eye[None, :, None, :, None] * w[:, :, :, None, :]).reshape(layers, nb * n, nb * n)


def kernel(x_prompt, x_sample, state_ret, state_lru_h, state_lru_conv, state_pool, w_ffn_gate, w_ffn_up, w_ffn_down, ln_g, ln_b, w_mix_in, w_mix_out, ret_gn_g, lru_conv_w, lru_conv_b, lru_wa, lru_ba, lru_wi, lru_bi, lru_lambda, pool_w, pool_b, pool_scale):
    batch, seq, _ = x_prompt.shape
    n_s = x_sample.shape[0]
    assert x_sample.shape[1] == 1 and seq % SEQ_TILE == 0 and n_s % SAMPLE_GROUP == 0
    xp = x_prompt.reshape(batch * seq, D_MODEL)
    xs = x_sample.reshape(n_s, D_MODEL)
    rows_of = lambda a: a.reshape(a.shape[:-1] + (1, a.shape[-1]))

    cos_p, sin_p = _rotary_tables(jnp.arange(seq))
    cos_s, sin_s = _rotary_tables(jnp.full((n_s,), PAST_LEN))

    w_in_all, w_out_all, pool_w_all = (w.astype(BF16) for w in (w_mix_in, w_mix_out, pool_w))
    ln_g_rows, ln_b_rows = rows_of(ln_g), rows_of(ln_b)
    ln = lambda layer, i: [_pick(ln_g_rows, layer, i), _pick(ln_b_rows, layer, i)]
    gn_rows, conv_b_rows, lam_rows = rows_of(ret_gn_g), rows_of(lru_conv_b), rows_of(lru_lambda)
    pool_b_rows, pool_scale_rows = rows_of(pool_b), rows_of(pool_scale)
    w_gates_all = jnp.concatenate([_block_diag(lru_wa), _block_diag(lru_wi)], axis=-1).astype(BF16)
    b_gates_rows = rows_of(jnp.concatenate([lru_ba, lru_bi], axis=-1))
    ffn_stacks = (w_ffn_gate, w_ffn_up, w_ffn_down)
    ffn_w = [w[0, 0].astype(BF16) for w in ffn_stacks]

    def ffn(pre_p, pre_s, rows_p, rows_s, pre_consts, layer, k, **kw):
        nonlocal ffn_w
        consts = ffn_w + ln(layer, 2 * k)
        nxt = (layer, 1) if k == 0 else (layer + 1, 0)
        cast_next = None if nxt[0] == DEPTH else (*ffn_stacks, *nxt)
        xp, xs, *cast = _ffn_pipeline(pre_p, pre_s, rows_p, rows_s, pre_consts, consts,
                                      cast_next=cast_next, **kw)
        ffn_w = cast
        return xp, xs

    rets_p, hs_p, convs_p, pools_p = [], [], [], []
    hs_s, convs_s, pools_s = [], [], []
    projs_s = []
    for layer in range(DEPTH):
        j = layer // 2
        xp, xs = ffn(_same(_identity_pre), _identity_pre, [xp], [xs], [], layer, 0)
        if layer % 2 == 0:
            gn = _pick(gn_rows, j)
            lru_consts = (_pick(lru_conv_w, j), _pick(conv_b_rows, j), _pick(w_gates_all, j),
                          _pick(b_gates_rows, j), _pick(lam_rows, j))
            out_consts = [_pick(w_out_all, j, rows=(RET_WIDTH, 0)), _pick(w_out_all, j, rows=(LRU_WIDTH, 1)),
                          *ln(layer, 1)]

            x1_s = xs
            xp, s_p, h_p, tail_p = _mixer_prompt(
                xp, cos_p, sin_p, [_pick(w_in_all, j), gn, *lru_consts, *out_consts], batch, seq)
            proj_s = _mix_in_sample(x1_s, cos_s, sin_s, _pick(w_in_all, j))
            projs_s.append(proj_s)
            ret_s, *rets_s = _ret_sample(projs_s, gn, state_ret, write_states=len(projs_s) == state_ret.shape[0])
            lru_s, h_s = _lru_sample(proj_s, state_lru_conv[j], _pick(state_lru_h, j), *lru_consts)

            rets_p.append(s_p)
            hs_p.append(h_p[:, 0, :])
            convs_p.append(tail_p[:, SUBLANES - (CONV_W - 1):, :])
            hs_s.append(h_s)
            ux_s = proj_s[:, MIX_ORDER.index("ux") * LRU_WIDTH:][:, :LRU_WIDTH]
            convs_s.append(jnp.concatenate([state_lru_conv[j][:, 1:, :], ux_s[:, None, :]], axis=1))

            xp, xs = ffn(_same(_identity_pre), _mix_out_ln, [xp], [x1_s, ret_s, lru_s], out_consts, layer, 1)
        else:
            pool_consts = [_pick(pool_w_all, j), _pick(pool_b_rows, j), _pick(pool_scale_rows, j), *ln(layer, 1)]
            pools_p.append(xp.reshape(batch, seq, D_MODEL)[:, seq - POOL_BUF:, :])
            pools_s.append(jnp.concatenate([state_pool[j][:, 1:, :], xs[:, None, :]], axis=1))
            xs = _pool_sample(xs, jnp.transpose(state_pool[j], (1, 0, 2)), pool_consts)
            xp, xs = ffn(functools.partial(_pool_ln_prompt, seq // ROW_TILE), lambda rows, consts: rows[0],
                         [xp, xp], [xs], pool_consts, layer, 1, row_leads=[0, 1],
                         scratch=[pltpu.VMEM((POOL_HALO, D_MODEL), F32)]
                         + [pltpu.VMEM((ROW_TILE, D_MODEL), BF16)] * 2)

    return (xp.reshape(batch, seq, D_MODEL), xs.reshape(n_s, 1, D_MODEL),
            jnp.stack(rets_p), jnp.stack(hs_p), jnp.stack(convs_p), jnp.stack(pools_p),
            rets_s[0], jnp.stack(hs_s), jnp.stack(convs_s), jnp.stack(pools_s))
```

```python
import functools
import math

import numpy as np
import jax
import jax.numpy as jnp
from jax import lax
from jax.experimental import pallas as pl
from jax.experimental.pallas import tpu as pltpu

F32 = jnp.float32
BF16 = jnp.bfloat16

D_MODEL = 1024
DEPTH = 4
PAST_LEN = 16384
RET_HEADS = 4
HEAD_DIM = 128
RET_WIDTH = RET_HEADS * HEAD_DIM
RET_CHUNK = 128
ROPE_BASE = 10000.0
LRU_WIDTH = 512
LRU_C = 8.0
CONV_W = 4
POOL_WINDOWS = (2, 4, 8, 16)
POOL_GROUP = D_MODEL // len(POOL_WINDOWS)
POOL_BUF = max(POOL_WINDOWS) - 1
LN_EPS = 1e-5
DN_ALPHA = (2.0 * DEPTH) ** 0.25

SUBLANES = 8
BF16_SUBLANES = 16
ROW_TILE = 512
SEQ_TILE = 1024
SAMPLE_GROUP = 16
CAST_BLOCKS = 32
VMEM_LIMIT = 56 * 1024 * 1024


def _sigmoid(x):
    return 0.5 * jnp.tanh(0.5 * x) + 0.5


def _silu(x):
    return x * _sigmoid(x)


def _gelu_tanh(x):
    return x * (0.5 * (1.0 + jnp.tanh(math.sqrt(2.0 / math.pi) * (x + 0.044715 * (x * x * x)))))


def _layer_norm(z, g, b):
    mu = jnp.mean(z, axis=-1, keepdims=True)
    d = z - mu
    var = jnp.mean(d * d, axis=-1, keepdims=True)
    return d * lax.rsqrt(var + LN_EPS) * g + b


def _unit_norm(z):
    mu = jnp.mean(z, axis=-1, keepdims=True)
    d = z - mu
    var = jnp.mean(d * d, axis=-1, keepdims=True)
    return d * lax.rsqrt(var + LN_EPS)


def _dot(a, b):
    return jnp.dot(a, b, preferred_element_type=F32)


def _resident(arr):
    nd = arr.ndim
    return pl.BlockSpec(arr.shape, lambda *_: (0,) * nd, pipeline_mode=pl.Buffered(1))


def _with_spec(const):
    return const if isinstance(const, tuple) else (const, _resident(const))


def _pick(arr, *lead, rows=None):
    tail = arr.shape[len(lead):]
    first = 0
    if rows is not None:
        tail = (rows[0],) + tail[1:]
        first = rows[1]
    index = tuple(lead) + (first,) + (0,) * (len(tail) - 1)
    return arr, pl.BlockSpec((None,) * len(lead) + tail, lambda *_: index, pipeline_mode=pl.Buffered(1))


def _same(body):
    return lambda tile, rows, consts, scratch, parity, first: (body([r[...] for r in rows], consts), ())


def _ffn_residual(x, wg, wu, wd, between=()):
    between = list(between) + [None, None]
    xb = x.astype(BF16)
    hg = _dot(xb, wg[...])
    if between[0] is not None:
        between[0]()
    hu = _dot(xb, wu[...])
    act = (_silu(hg) * hu).astype(BF16)
    if between[1] is not None:
        between[1]()
    return DN_ALPHA * x + 0.5 * _dot(act, wd[...])


def _ffn_pipeline(pre_p, pre_s, rows_p, rows_s, pre_consts, ffn_consts, scratch=(), cast_next=None,
                  row_leads=None):
    n_rows_p = rows_p[0].shape[0]
    n_rows_s = rows_s[0].shape[0]
    assert n_rows_p % ROW_TILE == 0
    n_p = n_rows_p // ROW_TILE
    n_in_p, n_in_s, n_pc = len(rows_p), len(rows_s), len(pre_consts)
    consts = [_with_spec(c) for c in list(pre_consts) + list(ffn_consts)]
    n_cast = 0 if cast_next is None else 3

    def kern(*refs):
        refs = list(refs)
        p_refs, refs = refs[:n_in_p], refs[n_in_p:]
        s_refs, refs = refs[:n_in_s], refs[n_in_s:]
        pc_refs, refs = refs[:n_pc], refs[n_pc:]
        (wg, wu, wd, ln_g, ln_b), refs = refs[:5], refs[5:]
        cast_in, refs = refs[:n_cast], refs[n_cast:]
        (out_p, out_s), refs = refs[:2], refs[2:]
        cast_out, refs = refs[:n_cast], refs[n_cast:]
        (x2_even, x2_odd, z_scr), scr = refs[:3], refs[3:]
        g = pl.program_id(0)
        steady = (g >= 1) & (g <= n_p)


        @pl.when(g == 0)
        def _():
            z_scr[...] = jnp.zeros_like(z_scr)
            x2, later = pre_p(g, p_refs, pc_refs, scr, 0, True)
            x2_even[...] = x2
            for work in later:
                work()

        def steady_step(x2_cur, x2_next, parity):
            later = []

            def after_gate():
                x2, work = pre_p(g, p_refs, pc_refs, scr, parity, False)
                x2_next[...] = x2
                later.extend(work)
                if later:
                    later.pop(0)()

            def before_down():
                if later:
                    later.pop(0)()
                out_p[...] = _layer_norm(z_scr[...], ln_g[...], ln_b[...])
                for src, dst in zip(cast_in, cast_out):
                    dst[...] = src[...].astype(dst.dtype)

            z_scr[...] = _ffn_residual(x2_cur[...], wg, wu, wd, (after_gate, before_down))

        pl.when(steady & (g % 2 == 1))(lambda: steady_step(x2_even, x2_odd, 1))
        pl.when(steady & (g % 2 == 0))(lambda: steady_step(x2_odd, x2_even, 0))

        @pl.when(g == n_p + 1)
        def _():
            out_p[...] = _layer_norm(z_scr[...], ln_g[...], ln_b[...])
            x2 = pre_s([r[...] for r in s_refs], pc_refs)
            out_s[...] = _layer_norm(_ffn_residual(x2, wg, wu, wd), ln_g[...], ln_b[...])

    tile = lambda cols, lag: pl.BlockSpec(
        (ROW_TILE, cols), lambda g: (jnp.clip(g - lag, 0, n_p - 1), 0))
    in_specs = ([tile(a.shape[1], -lead) for a, lead in zip(rows_p, row_leads or [0] * n_in_p)]
                + [pl.BlockSpec(a.shape, lambda g: (0, 0)) for a in rows_s]
                + [spec for _, spec in consts])
    out_specs = [tile(D_MODEL, 2), pl.BlockSpec((n_rows_s, D_MODEL), lambda g: (0, 0))]
    out_shape = [jax.ShapeDtypeStruct((n_rows_p, D_MODEL), F32), jax.ShapeDtypeStruct((n_rows_s, D_MODEL), F32)]
    cast_operands = []
    if cast_next is not None:
        *stacked, layer, k = cast_next
        for w in stacked:
            rows, cols = w.shape[2:]
            hold = max(1, n_p // CAST_BLOCKS)
            size = next(s for s in range(BF16_SUBLANES, rows + 1, BF16_SUBLANES)
                        if rows % s == 0 and rows // s * hold <= n_p)
            block = lambda g, last=rows // size - 1, hold=hold: jnp.clip((g - 1) // hold, 0, last)
            in_specs.append(pl.BlockSpec((None, None, size, cols), lambda g, b=block: (layer, k, b(g), 0)))
            out_specs.append(pl.BlockSpec((size, cols), lambda g, b=block: (b(g), 0)))
            out_shape.append(jax.ShapeDtypeStruct((rows, cols), BF16))
            cast_operands.append(w)
    return pl.pallas_call(
        kern, grid=(n_p + 2,), in_specs=in_specs, out_specs=out_specs, out_shape=out_shape,
        scratch_shapes=[pltpu.VMEM((ROW_TILE, D_MODEL), F32)] * 3 + list(scratch),
        compiler_params=pltpu.CompilerParams(dimension_semantics=("arbitrary",),
                                             vmem_limit_bytes=VMEM_LIMIT),
    )(*rows_p, *rows_s, *[a for a, _ in consts], *cast_operands)


def _identity_pre(rows, consts):
    (x,) = rows
    return x


def _mix_in_sample_kernel(x_ref, cos_ref, sin_ref, w_ref, o_ref):
    o_ref[...] = _rotary_qk(_dot(x_ref[...].astype(BF16), w_ref[...]), cos_ref[...], sin_ref[...])


def _mix_in_sample(x, cos, sin, w_in):
    full = lambda a: pl.BlockSpec(a.shape, lambda i: (0,) * a.ndim)
    out = jax.ShapeDtypeStruct((x.shape[0], MIX_COLS), F32)
    return pl.pallas_call(
        _mix_in_sample_kernel, grid=(1,), in_specs=[full(x), full(cos), full(sin), w_in[1]],
        out_specs=full(out), out_shape=out,
        compiler_params=pltpu.CompilerParams(vmem_limit_bytes=VMEM_LIMIT),
    )(x, cos, sin, w_in[0])


def _mix_out_ln(rows, consts):
    x, ret, lru = rows
    w_ret, w_lru, g, b = consts
    y = _dot(ret, w_ret[...]) + _dot(lru, w_lru[...])
    return _layer_norm(DN_ALPHA * x + y, g[...], b[...])


def _retention_tables():
    lg = np.log1p(-np.exp2(-5.0 - np.arange(RET_HEADS, dtype=np.float64)))
    idx = np.arange(RET_CHUNK, dtype=np.float64)
    diff = idx[:, None] - idx[None, :]
    decay = np.where(diff >= 0, np.exp(lg[:, None, None] * np.maximum(diff, 0.0)), 0.0)
    ones = np.ones((1, 1, HEAD_DIM))
    q_dec = np.exp(lg[:, None] * (idx + 1.0))[..., None] * ones
    k_dec = np.exp(lg[:, None] * (RET_CHUNK - 1.0 - idx))[..., None] * ones
    c_dec = np.exp(lg * RET_CHUNK)
    as32 = lambda a: jnp.asarray(a.astype(np.float32))
    return as32(decay), as32(q_dec), as32(k_dec), [float(c) for c in c_dec], [float(g) for g in np.exp(lg)]


def _ret_prompt_tile(c_dec, q_ref, k_ref, v_ref, g_ref, gn_ref, dec_ref, qd_ref, kd_ref, states):
    heads = range(RET_HEADS)
    head_cols = [slice(h * HEAD_DIM, (h + 1) * HEAD_DIM) for h in heads]
    states = list(states)
    out_rows = []
    for c in range(SEQ_TILE // RET_CHUNK):
        rows = slice(c * RET_CHUNK, (c + 1) * RET_CHUNK)
        out_heads = []
        q = [q_ref[rows, cols] for cols in head_cols]
        k = [k_ref[rows, cols] for cols in head_cols]
        vb = [v_ref[rows, cols].astype(BF16) for cols in head_cols]
        scores = [lax.dot_general(q[h].astype(BF16), k[h].astype(BF16), (((1,), (1,)), ((), ())),
                                  preferred_element_type=F32) for h in heads]
        update = [_dot((k[h] * kd_ref[h]).T.astype(BF16), vb[h]) for h in heads]
        for h in heads:
            lhs = jnp.concatenate([(scores[h] * dec_ref[h]).astype(BF16), (q[h] * qd_ref[h]).astype(BF16)], axis=1)
            o = _dot(lhs, jnp.concatenate([vb[h], states[h].astype(BF16)], axis=0))
            states[h] = states[h] * c_dec[h] + update[h]
            o = _unit_norm(o) * gn_ref[:, head_cols[h]]
            out_heads.append((_silu(g_ref[rows, head_cols[h]]) * o).astype(BF16))
        out_rows.append(jnp.concatenate(out_heads, axis=1))
    return jnp.concatenate(out_rows, axis=0), states


def _ret_sample_kernel(gamma, q_ref, k_ref, v_ref, g_ref, gn_ref, s_in, *rest):
    earlier, outs = rest[:len(rest) // 3 * 3], rest[len(rest) // 3 * 3:]
    o_ref = outs[0]
    s_out = outs[1] if len(outs) > 1 else None
    n = SAMPLE_GROUP
    row = lax.broadcasted_iota(jnp.int32, (n, n * HEAD_DIM), 0)
    blk = lax.broadcasted_iota(jnp.int32, (n, n * HEAD_DIM), 1) // HEAD_DIM
    own = row == blk

    def new_states(k_ref, v_ref, s_ref, h):
        cols = slice(h * HEAD_DIM, (h + 1) * HEAD_DIM)
        v_wide = jnp.where(own, jnp.concatenate([v_ref[:, cols]] * n, axis=1), 0.0).astype(BF16)
        outer = _dot(k_ref[:, cols].T.astype(BF16), v_wide)
        return [s_ref[r, h] * gamma[h] + outer[:, r * HEAD_DIM:(r + 1) * HEAD_DIM] for r in range(n)]

    for h in range(RET_HEADS):
        cols = slice(h * HEAD_DIM, (h + 1) * HEAD_DIM)
        for slot in range(len(earlier) // 3):
            for r, s_new in enumerate(new_states(*earlier[3 * slot:3 * slot + 3], h)):
                s_out[slot, r, h] = s_new
        new = new_states(k_ref, v_ref, s_in, h)
        if s_out is not None:
            for r, s_new in enumerate(new):
                s_out[len(earlier) // 3, r, h] = s_new
        q = q_ref[:, cols]
        wide = _dot(q.astype(BF16), jnp.concatenate([s.astype(BF16) for s in new], axis=1))
        wide = jnp.where(own, wide, 0.0)
        o = wide[:, :HEAD_DIM]
        for r in range(1, n):
            o = o + wide[:, r * HEAD_DIM:(r + 1) * HEAD_DIM]
        o = _unit_norm(o) * gn_ref[:, cols]
        o_ref[:, cols] = (_silu(g_ref[:, cols]) * o).astype(o_ref.dtype)


def _ret_sample(projs, gn_g, states, write_states):
    _, _, _, _, gamma = _retention_tables()
    j = len(projs) - 1
    n_rows = projs[-1].shape[0]
    n = SAMPLE_GROUP
    col = lambda name: pl.BlockSpec((n, RET_WIDTH), lambda i: (i, MIX_ORDER.index(name)))
    state = lambda slot: pl.BlockSpec((None, n, RET_HEADS, HEAD_DIM, HEAD_DIM), lambda i: (slot, i, 0, 0, 0))
    gn_g, gn_spec = _with_spec(gn_g)
    operands = [projs[-1]] * 4 + [gn_g, states]
    in_specs = [col("q"), col("k"), col("v"), col("g"), gn_spec, state(j)]
    out_specs = [pl.BlockSpec((n, RET_WIDTH), lambda i: (i, 0))]
    out_shape = [jax.ShapeDtypeStruct((n_rows, RET_WIDTH), BF16)]
    if write_states:
        for slot, proj in enumerate(projs[:-1]):
            operands += [proj, proj, states]
            in_specs += [col("k"), col("v"), state(slot)]
        out_specs.append(pl.BlockSpec((j + 1, n, RET_HEADS, HEAD_DIM, HEAD_DIM), lambda i: (0, i, 0, 0, 0)))
        out_shape.append(jax.ShapeDtypeStruct((j + 1,) + states.shape[1:], F32))
    return pl.pallas_call(
        functools.partial(_ret_sample_kernel, gamma),
        grid=(n_rows // n,),
        in_specs=in_specs, out_specs=out_specs, out_shape=out_shape,
        compiler_params=pltpu.CompilerParams(dimension_semantics=("parallel",),
                                             vmem_limit_bytes=VMEM_LIMIT),
    )(*operands)


def _lru_gates(uc, w_gates, b_gates, lam):
    pre = _dot(uc.astype(BF16), w_gates) + b_gates
    r = _sigmoid(pre[:, :LRU_WIDTH])
    i = _sigmoid(pre[:, LRU_WIDTH:])
    neg = -lam
    softplus = jnp.maximum(neg, 0.0) + jnp.log1p(jnp.exp(-jnp.abs(neg)))
    log_a = (-LRU_C * softplus) * r
    a = jnp.exp(log_a)
    xin = jnp.sqrt(jnp.tanh(-log_a) * (1.0 + a * a)) * (i * uc)
    return a, xin


def _lru_prompt_tile(u, ug, cw_ref, cb_ref, wg_ref, bg_ref, lam_ref, h, before, ext):
    n = SEQ_TILE
    pad = SUBLANES
    ext[0:pad, :] = before
    ext[pad:pad + n, :] = u
    uc = cb_ref[...] + cw_ref[CONV_W - 1:CONV_W, :] * u
    for j in range(1, CONV_W):
        uc = uc + cw_ref[CONV_W - 1 - j:CONV_W - j, :] * ext[pad - j:pad - j + n, :]

    a, x = _lru_gates(uc, wg_ref[...], bg_ref[...], lam_ref[...])

    groups = (n // SUBLANES, SUBLANES, LRU_WIDTH)
    a = a.reshape(groups)
    x = x.reshape(groups)
    step = lax.broadcasted_iota(jnp.int32, groups, 1)
    shift = 1
    while shift < SUBLANES:
        inside = step >= shift
        x = a * jnp.where(inside, pltpu.roll(x, shift, 1), 0.0) + x
        a = a * jnp.where(inside, pltpu.roll(a, shift, 1), 1.0)
        shift *= 2
    hs = []
    for i in range(n // SUBLANES):
        hg = a[i] * h + x[i]
        hs.append(hg)
        h = jnp.broadcast_to(hg[SUBLANES - 1:SUBLANES, :], (SUBLANES, LRU_WIDTH))
    return jnp.concatenate(hs, axis=0) * _gelu_tanh(ug), h, u[n - pad:, :]


QKVG_COLS = 2 * RET_WIDTH * 2
MIX_COLS = QKVG_COLS + 2 * LRU_WIDTH
MIX_ORDER = ("q", "k", "v", "g", "ux", "ug")
QK_START = MIX_ORDER.index("q") * RET_WIDTH
VG_START = MIX_ORDER.index("v") * RET_WIDTH
LRU_START = MIX_ORDER.index("ux") * RET_WIDTH


def _rotary_heads(qk, cos, sin):
    pieces = []
    for j in range(2 * RET_HEADS):
        blk = qk[:, j * HEAD_DIM:(j + 1) * HEAD_DIM]
        rot = blk * cos + pltpu.roll(blk, HEAD_DIM // 2, 1) * sin
        if j < RET_HEADS:
            rot = rot * (HEAD_DIM ** -0.5)
        pieces.append(rot)
    return jnp.concatenate(pieces, axis=1)


def _rotary_qk(proj, cos, sin):
    assert QK_START == 0
    return jnp.concatenate([_rotary_heads(proj[:, :VG_START], cos, sin), proj[:, VG_START:]], axis=1)


def _mixer_prompt_kernel(c_dec, x_ref, cos_ref, sin_ref, w_in, gn_ref, dec_ref, qd_ref, kd_ref,
                         cw_ref, cb_ref, wg_ref, bg_ref, lam_ref, w_ret, w_lru, ln_g, ln_b,
                         x2_ref, s_ref, h_ref, tail_ref, ext):
    heads = range(RET_HEADS)

    @pl.when(pl.program_id(1) == 0)
    def _():
        s_ref[...] = jnp.zeros_like(s_ref)
        h_ref[...] = jnp.zeros_like(h_ref)
        tail_ref[...] = jnp.zeros_like(tail_ref)

    x = x_ref[...]
    xb = x.astype(BF16)
    lru_in = _dot(xb, w_in[:, LRU_START:])
    qk_raw = _dot(xb, w_in[:, QK_START:VG_START])
    lru_out, h_new, tail = _lru_prompt_tile(lru_in[:, :LRU_WIDTH], lru_in[:, LRU_WIDTH:], cw_ref, cb_ref, wg_ref,
                                            bg_ref, lam_ref, h_ref[0], tail_ref[0], ext)
    vg = _dot(xb, w_in[:, VG_START:LRU_START])
    qk = _rotary_heads(qk_raw, cos_ref[...], sin_ref[...])
    ret_out, states = _ret_prompt_tile(
        c_dec, qk[:, :RET_WIDTH], qk[:, RET_WIDTH:], vg[:, :RET_WIDTH], vg[:, RET_WIDTH:],
        gn_ref, dec_ref, qd_ref, kd_ref, [s_ref[0, h] for h in heads])
    h_ref[0] = h_new
    tail_ref[0] = tail
    for h in heads:
        s_ref[0, h] = states[h]
    y = _dot(ret_out, w_ret[...]) + _dot(lru_out.astype(BF16), w_lru[...])
    x2_ref[...] = _layer_norm(DN_ALPHA * x + y, ln_g[...], ln_b[...])


def _mixer_prompt(x, cos, sin, consts, batch, seq):
    decay, q_dec, k_dec, c_dec, _ = _retention_tables()
    w_in, gn_g, *rest = consts
    consts = [w_in, gn_g, decay, q_dec, k_dec] + rest
    consts = [_with_spec(c) for c in consts]
    n_t = seq // SEQ_TILE
    tile = pl.BlockSpec((SEQ_TILE, D_MODEL), lambda b, t: (b * n_t + t, 0))
    table = pl.BlockSpec((SEQ_TILE, HEAD_DIM), lambda b, t: (t, 0))
    small = pl.BlockSpec((1, SUBLANES, LRU_WIDTH), lambda b, t: (b, 0, 0))
    return pl.pallas_call(
        functools.partial(_mixer_prompt_kernel, c_dec),
        grid=(batch, n_t),
        in_specs=[tile, table, table] + [spec for _, spec in consts],
        out_specs=[tile, pl.BlockSpec((1, RET_HEADS, HEAD_DIM, HEAD_DIM), lambda b, t: (b, 0, 0, 0)),
                   small, small],
        out_shape=[jax.ShapeDtypeStruct((batch * seq, D_MODEL), F32),
                   jax.ShapeDtypeStruct((batch, RET_HEADS, HEAD_DIM, HEAD_DIM), F32),
                   jax.ShapeDtypeStruct((batch, SUBLANES, LRU_WIDTH), F32),
                   jax.ShapeDtypeStruct((batch, SUBLANES, LRU_WIDTH), F32)],
        scratch_shapes=[pltpu.VMEM((SEQ_TILE + SUBLANES, LRU_WIDTH), F32)],
        compiler_params=pltpu.CompilerParams(dimension_semantics=("parallel", "arbitrary"),
                                             vmem_limit_bytes=VMEM_LIMIT),
    )(x, cos, sin, *[a for a, _ in consts])


def _lru_sample_kernel(ux_ref, ug_ref, b0_ref, b1_ref, b2_ref, h0_ref, cw_ref, cb_ref, wg_ref, bg_ref,
                       lam_ref, o_ref, h_ref):
    u = ux_ref[...]
    uc = (cb_ref[...] + cw_ref[3:4, :] * u + cw_ref[2:3, :] * b2_ref[...]
          + cw_ref[1:2, :] * b1_ref[...] + cw_ref[0:1, :] * b0_ref[...])
    a, x = _lru_gates(uc, wg_ref[...], bg_ref[...], lam_ref[...])
    h = a * h0_ref[...] + x
    h_ref[...] = h
    o_ref[...] = (h * _gelu_tanh(ug_ref[...])).astype(o_ref.dtype)


def _lru_sample(proj, conv_buf, h0, conv_w, conv_b, w_gates, b_gates, lam):
    n_rows = proj.shape[0]
    ux_col = MIX_ORDER.index("ux")
    col = lambda j: pl.BlockSpec((n_rows, LRU_WIDTH), lambda i: (0, j))
    bufs = [conv_buf[:, j, :] for j in range(CONV_W - 1)]
    rest = [_with_spec(c) for c in bufs + [h0, conv_w, conv_b, w_gates, b_gates, lam]]
    return pl.pallas_call(
        _lru_sample_kernel,
        grid=(1,),
        in_specs=[col(ux_col), col(ux_col + 1)] + [spec for _, spec in rest],
        out_specs=[pl.BlockSpec((n_rows, LRU_WIDTH), lambda i: (0, 0))] * 2,
        out_shape=[jax.ShapeDtypeStruct((n_rows, LRU_WIDTH), BF16),
                   jax.ShapeDtypeStruct((n_rows, LRU_WIDTH), F32)],
        compiler_params=pltpu.CompilerParams(vmem_limit_bytes=VMEM_LIMIT),
    )(proj, proj, *[a for a, _ in rest])


def _pool_finish(x, deltas, w_ref, pb_ref, ps_ref, g_ref, b_ref):
    ys = [_dot(deltas[:, gi * POOL_GROUP:(gi + 1) * POOL_GROUP], w_ref[gi]) for gi in range(len(POOL_WINDOWS))]
    y = (jnp.concatenate(ys, axis=1) + pb_ref[...]) * ps_ref[...]
    return _layer_norm(DN_ALPHA * x + y, g_ref[...], b_ref[...])


POOL_HALO = POOL_BUF + 1


POOL_ROWS = 64
assert all(w & (w - 1) == 0 for w in POOL_WINDOWS)


def _pool_deltas(tiles_per_seq, tile, x_ref, halo, out, groups):
    n = ROW_TILE
    t = tile % tiles_per_seq
    for gi in groups:
        wnd = POOL_WINDOWS[gi]
        cols = slice(gi * POOL_GROUP, (gi + 1) * POOL_GROUP)
        for r0 in range(0, n, POOL_ROWS):
            rows = slice(r0, r0 + POOL_ROWS)
            if r0 == 0:
                before = jnp.where(t == 0, 0.0, halo[:, cols])
                w = jnp.concatenate([before, x_ref[rows, cols]], axis=0)
            else:
                w = x_ref[r0 - POOL_HALO:r0 + POOL_ROWS, cols]
            shift = 1
            while shift < wnd:
                w = w + pltpu.roll(w, shift, 0)
                shift *= 2
            if r0 + 1 >= wnd:
                mean = w[POOL_HALO:, :] * (1.0 / wnd)
            else:
                pos = t * n + r0 + lax.broadcasted_iota(jnp.int32, (POOL_ROWS, POOL_GROUP), 0)
                mean = w[POOL_HALO:, :] / jnp.minimum(wnd, pos + 1).astype(F32)
            out[rows, cols] = (mean - x_ref[rows, cols]).astype(BF16)
        halo[:, cols] = x_ref[n - POOL_HALO:, cols]


def _pool_ln_prompt(tiles_per_seq, tile, rows, consts, scratch, parity, first):
    x_ref, x_next_ref = rows
    halo, stash_even, stash_odd = scratch
    stash = (stash_even, stash_odd)
    all_groups = range(len(POOL_WINDOWS))
    if first:
        halo[...] = jnp.zeros_like(halo)
        _pool_deltas(tiles_per_seq, tile, x_ref, halo, stash[parity], all_groups)
    x2 = _pool_finish(x_ref[...], stash[parity][...], *consts)
    ahead = functools.partial(_pool_deltas, tiles_per_seq, tile + 1, x_next_ref, halo, stash[1 - parity])
    return x2, (lambda: ahead(all_groups[2:]), lambda: ahead(all_groups[:2]))


def _pool_sample_kernel(x_ref, buf_ref, w_ref, pb_ref, ps_ref, g_ref, b_ref, o_ref):
    x = x_ref[...]
    deltas = []
    for gi, wnd in enumerate(POOL_WINDOWS):
        cols = slice(gi * POOL_GROUP, (gi + 1) * POOL_GROUP)
        s = x[:, cols]
        for j in range(1, wnd):
            s = s + buf_ref[POOL_BUF - j, :, cols]
        cnt = min(float(wnd), PAST_LEN + 1.0)
        deltas.append((s / cnt - x[:, cols]).astype(BF16))
    o_ref[...] = _pool_finish(x, jnp.concatenate(deltas, axis=1), w_ref, pb_ref, ps_ref, g_ref, b_ref)


def _pool_sample(x, buf_t, consts):
    full = lambda a: pl.BlockSpec(a.shape, lambda i: (0,) * a.ndim)
    consts = [_with_spec(c) for c in consts]
    return pl.pallas_call(
        _pool_sample_kernel,
        grid=(1,),
        in_specs=[full(x), full(buf_t)] + [spec for _, spec in consts],
        out_specs=full(x),
        out_shape=jax.ShapeDtypeStruct(x.shape, F32),
        compiler_params=pltpu.CompilerParams(vmem_limit_bytes=VMEM_LIMIT),
    )(x, buf_t, *[a for a, _ in consts])


def _rotary_tables(positions):
    half = HEAD_DIM // 2
    inv = ROPE_BASE ** (-jnp.arange(half, dtype=F32) / half)
    ang = positions.astype(F32)[:, None] * inv[None, :]
    cos, sin = jnp.cos(ang), jnp.sin(ang)
    return jnp.concatenate([cos, cos], axis=1), jnp.concatenate([-sin, sin], axis=1)


def _block_diag(w):
    layers, nb, n, _ = w.shape
    eye = jnp.eye(nb, dtype=w.dtype)
    return (eye[None, :, None, :, None] * w[:, :, :, None, :]).reshape(layers, nb * n, nb * n)


def kernel(x_prompt, x_sample, state_ret, state_lru_h, state_lru_conv, state_pool, w_ffn_gate, w_ffn_up, w_ffn_down, ln_g, ln_b, w_mix_in, w_mix_out, ret_gn_g, lru_conv_w, lru_conv_b, lru_wa, lru_ba, lru_wi, lru_bi, lru_lambda, pool_w, pool_b, pool_scale):
    batch, seq, _ = x_prompt.shape
    n_s = x_sample.shape[0]
    assert x_sample.shape[1] == 1 and seq % SEQ_TILE == 0 and n_s % SAMPLE_GROUP == 0
    xp = x_prompt.reshape(batch * seq, D_MODEL)
    xs = x_sample.reshape(n_s, D_MODEL)
    rows_of = lambda a: a.reshape(a.shape[:-1] + (1, a.shape[-1]))

    cos_p, sin_p = _rotary_tables(jnp.arange(seq))
    cos_s, sin_s = _rotary_tables(jnp.full((n_s,), PAST_LEN))

    w_in_all, w_out_all, pool_w_all = (w.astype(BF16) for w in (w_mix_in, w_mix_out, pool_w))
    ln_g_rows, ln_b_rows = rows_of(ln_g), rows_of(ln_b)
    ln = lambda layer, i: [_pick(ln_g_rows, layer, i), _pick(ln_b_rows, layer, i)]
    gn_rows, conv_b_rows, lam_rows = rows_of(ret_gn_g), rows_of(lru_conv_b), rows_of(lru_lambda)
    pool_b_rows, pool_scale_rows = rows_of(pool_b), rows_of(pool_scale)
    w_gates_all = jnp.concatenate([_block_diag(lru_wa), _block_diag(lru_wi)], axis=-1).astype(BF16)
    b_gates_rows = rows_of(jnp.concatenate([lru_ba, lru_bi], axis=-1))
    ffn_stacks = (w_ffn_gate, w_ffn_up, w_ffn_down)
    ffn_w = [w[0, 0].astype(BF16) for w in ffn_stacks]

    def ffn(pre_p, pre_s, rows_p, rows_s, pre_consts, layer, k, **kw):
        nonlocal ffn_w
        consts = ffn_w + ln(layer, 2 * k)
        nxt = (layer, 1) if k == 0 else (layer + 1, 0)
        cast_next = None if nxt[0] == DEPTH else (*ffn_stacks, *nxt)
        xp, xs, *cast = _ffn_pipeline(pre_p, pre_s, rows_p, rows_s, pre_consts, consts,
                                      cast_next=cast_next, **kw)
        ffn_w = cast
        return xp, xs

    rets_p, hs_p, convs_p, pools_p = [], [], [], []
    hs_s, convs_s, pools_s = [], [], []
    projs_s = []
    for layer in range(DEPTH):
        j = layer // 2
        xp, xs = ffn(_same(_identity_pre), _identity_pre, [xp], [xs], [], layer, 0)
        if layer % 2 == 0:
            gn = _pick(gn_rows, j)
            lru_consts = (_pick(lru_conv_w, j), _pick(conv_b_rows, j), _pick(w_gates_all, j),
                          _pick(b_gates_rows, j), _pick(lam_rows, j))
            out_consts = [_pick(w_out_all, j, rows=(RET_WIDTH, 0)), _pick(w_out_all, j, rows=(LRU_WIDTH, 1)),
                          *ln(layer, 1)]

            x1_s = xs
            xp, s_p, h_p, tail_p = _mixer_prompt(
                xp, cos_p, sin_p, [_pick(w_in_all, j), gn, *lru_consts, *out_consts], batch, seq)
            proj_s = _mix_in_sample(x1_s, cos_s, sin_s, _pick(w_in_all, j))
            projs_s.append(proj_s)
            ret_s, *rets_s = _ret_sample(projs_s, gn, state_ret, write_states=len(projs_s) == state_ret.shape[0])
            lru_s, h_s = _lru_sample(proj_s, state_lru_conv[j], _pick(state_lru_h, j), *lru_consts)

            rets_p.append(s_p)
            hs_p.append(h_p[:, 0, :])
            convs_p.append(tail_p[:, SUBLANES - (CONV_W - 1):, :])
            hs_s.append(h_s)
            ux_s = proj_s[:, MIX_ORDER.index("ux") * LRU_WIDTH:][:, :LRU_WIDTH]
            convs_s.append(jnp.concatenate([state_lru_conv[j][:, 1:, :], ux_s[:, None, :]], axis=1))

            xp, xs = ffn(_same(_identity_pre), _mix_out_ln, [xp], [x1_s, ret_s, lru_s], out_consts, layer, 1)
        else:
            pool_consts = [_pick(pool_w_all, j), _pick(pool_b_rows, j), _pick(pool_scale_rows, j), *ln(layer, 1)]
            pools_p.append(xp.reshape(batch, seq, D_MODEL)[:, seq - POOL_BUF:, :])
            pools_s.append(jnp.concatenate([state_pool[j][:, 1:, :], xs[:, None, :]], axis=1))
            xs = _pool_sample(xs, jnp.transpose(state_pool[j], (1, 0, 2)), pool_consts)
            xp, xs = ffn(functools.partial(_pool_ln_prompt, seq // ROW_TILE), lambda rows, consts: rows[0],
                         [xp, xp], [xs], pool_consts, layer, 1, row_leads=[0, 1],
                         scratch=[pltpu.VMEM((POOL_HALO, D_MODEL), F32)]
                         + [pltpu.VMEM((ROW_TILE, D_MODEL), BF16)] * 2)

    return (xp.reshape(batch, seq, D_MODEL), xs.reshape(n_s, 1, D_MODEL),
            jnp.stack(rets_p), jnp.stack(hs_p), jnp.stack(convs_p), jnp.stack(pools_p),
            rets_s[0], jnp.stack(hs_s), jnp.stack(convs_s), jnp.stack(pools_s))
```

```python
import functools
import math

import numpy as np
import jax
import jax.numpy as jnp
from jax import lax
from jax.experimental import pallas as pl
from jax.experimental.pallas import tpu as pltpu

F32 = jnp.float32
BF16 = jnp.bfloat16

D_MODEL = 1024
DEPTH = 4
PAST_LEN = 16384
RET_HEADS = 4
HEAD_DIM = 128
RET_WIDTH = RET_HEADS * HEAD_DIM
RET_CHUNK = 128
ROPE_BASE = 10000.0
LRU_WIDTH = 512
LRU_C = 8.0
CONV_W = 4
POOL_WINDOWS = (2, 4, 8, 16)
POOL_GROUP = D_MODEL // len(POOL_WINDOWS)
POOL_BUF = max(POOL_WINDOWS) - 1
LN_EPS = 1e-5
DN_ALPHA = (2.0 * DEPTH) ** 0.25

SUBLANES = 8
BF16_SUBLANES = 16
ROW_TILE = 512
SEQ_TILE = 1024
SAMPLE_GROUP = 16
CAST_BLOCKS = 32
VMEM_LIMIT = 56 * 1024 * 1024


def _sigmoid(x):
    return 0.5 * jnp.tanh(0.5 * x) + 0.5


def _silu(x):
    return x * _sigmoid(x)


def _gelu_tanh(x):
    return x * (0.5 * (1.0 + jnp.tanh(math.sqrt(2.0 / math.pi) * (x + 0.044715 * (x * x * x)))))


def _layer_norm(z, g, b):
    mu = jnp.mean(z, axis=-1, keepdims=True)
    d = z - mu
    var = jnp.mean(d * d, axis=-1, keepdims=True)
    return d * lax.rsqrt(var + LN_EPS) * g + b


def _unit_norm(z):
    mu = jnp.mean(z, axis=-1, keepdims=True)
    d = z - mu
    var = jnp.mean(d * d, axis=-1, keepdims=True)
    return d * lax.rsqrt(var + LN_EPS)


def _dot(a, b):
    return jnp.dot(a, b, preferred_element_type=F32)


def _resident(arr):
    nd = arr.ndim
    return pl.BlockSpec(arr.shape, lambda *_: (0,) * nd, pipeline_mode=pl.Buffered(1))


def _with_spec(const):
    return const if isinstance(const, tuple) else (const, _resident(const))


def _pick(arr, *lead, rows=None):
    tail = arr.shape[len(lead):]
    first = 0
    if rows is not None:
        tail = (rows[0],) + tail[1:]
        first = rows[1]
    index = tuple(lead) + (first,) + (0,) * (len(tail) - 1)
    return arr, pl.BlockSpec((None,) * len(lead) + tail, lambda *_: index, pipeline_mode=pl.Buffered(1))


def _same(body):
    return lambda tile, rows, consts, scratch, parity, first: (body([r[...] for r in rows], consts), ())


def _ffn_residual(x, wg, wu, wd, between=()):
    between = list(between) + [None, None]
    xb = x.astype(BF16)
    hg = _dot(xb, wg[...])
    if between[0] is not None:
        between[0]()
    hu = _dot(xb, wu[...])
    act = (_silu(hg) * hu).astype(BF16)
    if between[1] is not None:
        between[1]()
    return DN_ALPHA * x + 0.5 * _dot(act, wd[...])


def _ffn_pipeline(pre_p, pre_s, rows_p, rows_s, pre_consts, ffn_consts, scratch=(), cast_next=None,
                  row_leads=None):
    n_rows_p = rows_p[0].shape[0]
    n_rows_s = rows_s[0].shape[0]
    assert n_rows_p % ROW_TILE == 0
    n_p = n_rows_p // ROW_TILE
    n_in_p, n_in_s, n_pc = len(rows_p), len(rows_s), len(pre_consts)
    consts = [_with_spec(c) for c in list(pre_consts) + list(ffn_consts)]
    n_cast = 0 if cast_next is None else 3

    def kern(*refs):
        refs = list(refs)
        p_refs, refs = refs[:n_in_p], refs[n_in_p:]
        s_refs, refs = refs[:n_in_s], refs[n_in_s:]
        pc_refs, refs = refs[:n_pc], refs[n_pc:]
        (wg, wu, wd, ln_g, ln_b), refs = refs[:5], refs[5:]
        cast_in, refs = refs[:n_cast], refs[n_cast:]
        (out_p, out_s), refs = refs[:2], refs[2:]
        cast_out, refs = refs[:n_cast], refs[n_cast:]
        (x2_even, x2_odd, z_scr), scr = refs[:3], refs[3:]
        g = pl.program_id(0)
        steady = (g >= 1) & (g <= n_p)


        @pl.when(g == 0)
        def _():
            z_scr[...] = jnp.zeros_like(z_scr)
            x2, later = pre_p(g, p_refs, pc_refs, scr, 0, True)
            x2_even[...] = x2
            for work in later:
                work()

        def steady_step(x2_cur, x2_next, parity):
            later = []

            def after_gate():
                x2, work = pre_p(g, p_refs, pc_refs, scr, parity, False)
                x2_next[...] = x2
                later.extend(work)
                if later:
                    later.pop(0)()

            def before_down():
                if later:
                    later.pop(0)()
                out_p[...] = _layer_norm(z_scr[...], ln_g[...], ln_b[...])
                for src, dst in zip(cast_in, cast_out):
                    dst[...] = src[...].astype(dst.dtype)

            z_scr[...] = _ffn_residual(x2_cur[...], wg, wu, wd, (after_gate, before_down))

        pl.when(steady & (g % 2 == 1))(lambda: steady_step(x2_even, x2_odd, 1))
        pl.when(steady & (g % 2 == 0))(lambda: steady_step(x2_odd, x2_even, 0))

        @pl.when(g == n_p + 1)
        def _():
            out_p[...] = _layer_norm(z_scr[...], ln_g[...], ln_b[...])
            x2 = pre_s([r[...] for r in s_refs], pc_refs)
            out_s[...] = _layer_norm(_ffn_residual(x2, wg, wu, wd), ln_g[...], ln_b[...])

    tile = lambda cols, lag: pl.BlockSpec(
        (ROW_TILE, cols), lambda g: (jnp.clip(g - lag, 0, n_p - 1), 0))
    in_specs = ([tile(a.shape[1], -lead) for a, lead in zip(rows_p, row_leads or [0] * n_in_p)]
                + [pl.BlockSpec(a.shape, lambda g: (0, 0)) for a in rows_s]
                + [spec for _, spec in consts])
    out_specs = [tile(D_MODEL, 2), pl.BlockSpec((n_rows_s, D_MODEL), lambda g: (0, 0))]
    out_shape = [jax.ShapeDtypeStruct((n_rows_p, D_MODEL), F32), jax.ShapeDtypeStruct((n_rows_s, D_MODEL), F32)]
    cast_operands = []
    if cast_next is not None:
        *stacked, layer, k = cast_next
        for w in stacked:
            rows, cols = w.shape[2:]
            hold = max(1, n_p // CAST_BLOCKS)
            size = next(s for s in range(BF16_SUBLANES, rows + 1, BF16_SUBLANES)
                        if rows % s == 0 and rows // s * hold <= n_p)
            block = lambda g, last=rows // size - 1, hold=hold: jnp.clip((g - 1) // hold, 0, last)
            in_specs.append(pl.BlockSpec((None, None, size, cols), lambda g, b=block: (layer, k, b(g), 0)))
            out_specs.append(pl.BlockSpec((size, cols), lambda g, b=block: (b(g), 0)))
            out_shape.append(jax.ShapeDtypeStruct((rows, cols), BF16))
            cast_operands.append(w)
    return pl.pallas_call(
        kern, grid=(n_p + 2,), in_specs=in_specs, out_specs=out_specs, out_shape=out_shape,
        scratch_shapes=[pltpu.VMEM((ROW_TILE, D_MODEL), F32)] * 3 + list(scratch),
        compiler_params=pltpu.CompilerParams(dimension_semantics=("arbitrary",),
                                             vmem_limit_bytes=VMEM_LIMIT),
    )(*rows_p, *rows_s, *[a for a, _ in consts], *cast_operands)


def _identity_pre(rows, consts):
    (x,) = rows
    return x


def _mix_in_sample_kernel(x_ref, cos_ref, sin_ref, w_ref, o_ref):
    o_ref[...] = _rotary_qk(_dot(x_ref[...].astype(BF16), w_ref[...]), cos_ref[...], sin_ref[...])


def _mix_in_sample(x, cos, sin, w_in):
    full = lambda a: pl.BlockSpec(a.shape, lambda i: (0,) * a.ndim)
    out = jax.ShapeDtypeStruct((x.shape[0], MIX_COLS), F32)
    return pl.pallas_call(
        _mix_in_sample_kernel, grid=(1,), in_specs=[full(x), full(cos), full(sin), w_in[1]],
        out_specs=full(out), out_shape=out,
        compiler_params=pltpu.CompilerParams(vmem_limit_bytes=VMEM_LIMIT),
    )(x, cos, sin, w_in[0])


def _mix_out_ln(rows, consts):
    x, ret, lru = rows
    w_ret, w_lru, g, b = consts
    y = _dot(ret, w_ret[...]) + _dot(lru, w_lru[...])
    return _layer_norm(DN_ALPHA * x + y, g[...], b[...])


def _retention_tables():
    lg = np.log1p(-np.exp2(-5.0 - np.arange(RET_HEADS, dtype=np.float64)))
    idx = np.arange(RET_CHUNK, dtype=np.float64)
    diff = idx[:, None] - idx[None, :]
    decay = np.where(diff >= 0, np.exp(lg[:, None, None] * np.maximum(diff, 0.0)), 0.0)
    ones = np.ones((1, 1, HEAD_DIM))
    q_dec = np.exp(lg[:, None] * (idx + 1.0))[..., None] * ones
    k_dec = np.exp(lg[:, None] * (RET_CHUNK - 1.0 - idx))[..., None] * ones
    c_dec = np.exp(lg * RET_CHUNK)
    as32 = lambda a: jnp.asarray(a.astype(np.float32))
    return as32(decay), as32(q_dec), as32(k_dec), [float(c) for c in c_dec], [float(g) for g in np.exp(lg)]


def _ret_prompt_tile(c_dec, q_ref, k_ref, v_ref, g_ref, gn_ref, dec_ref, qd_ref, kd_ref, states):
    heads = range(RET_HEADS)
    head_cols = [slice(h * HEAD_DIM, (h + 1) * HEAD_DIM) for h in heads]
    states = list(states)
    out_rows = []
    for c in range(SEQ_TILE // RET_CHUNK):
        rows = slice(c * RET_CHUNK, (c + 1) * RET_CHUNK)
        out_heads = []
        q = [q_ref[rows, cols] for cols in head_cols]
        k = [k_ref[rows, cols] for cols in head_cols]
        vb = [v_ref[rows, cols].astype(BF16) for cols in head_cols]
        scores = [lax.dot_general(q[h].astype(BF16), k[h].astype(BF16), (((1,), (1,)), ((), ())),
                                  preferred_element_type=F32) for h in heads]
        update = [_dot((k[h] * kd_ref[h]).T.astype(BF16), vb[h]) for h in heads]
        for h in heads:
            lhs = jnp.concatenate([(scores[h] * dec_ref[h]).astype(BF16), (q[h] * qd_ref[h]).astype(BF16)], axis=1)
            o = _dot(lhs, jnp.concatenate([vb[h], states[h].astype(BF16)], axis=0))
            states[h] = states[h] * c_dec[h] + update[h]
            o = _unit_norm(o) * gn_ref[:, head_cols[h]]
            out_heads.append((_silu(g_ref[rows, head_cols[h]]) * o).astype(BF16))
        out_rows.append(jnp.concatenate(out_heads, axis=1))
    return jnp.concatenate(out_rows, axis=0), states


def _ret_sample_kernel(gamma, q_ref, k_ref, v_ref, g_ref, gn_ref, s_in, *rest):
    earlier, outs = rest[:len(rest) // 3 * 3], rest[len(rest) // 3 * 3:]
    o_ref = outs[0]
    s_out = outs[1] if len(outs) > 1 else None
    n = SAMPLE_GROUP
    row = lax.broadcasted_iota(jnp.int32, (n, n * HEAD_DIM), 0)
    blk = lax.broadcasted_iota(jnp.int32, (n, n * HEAD_DIM), 1) // HEAD_DIM
    own = row == blk

    def new_states(k_ref, v_ref, s_ref, h):
        cols = slice(h * HEAD_DIM, (h + 1) * HEAD_DIM)
        v_wide = jnp.where(own, jnp.concatenate([v_ref[:, cols]] * n, axis=1), 0.0).astype(BF16)
        outer = _dot(k_ref[:, cols].T.astype(BF16), v_wide)
        return [s_ref[r, h] * gamma[h] + outer[:, r * HEAD_DIM:(r + 1) * HEAD_DIM] for r in range(n)]

    for h in range(RET_HEADS):
        cols = slice(h * HEAD_DIM, (h + 1) * HEAD_DIM)
        for slot in range(len(earlier) // 3):
            for r, s_new in enumerate(new_states(*earlier[3 * slot:3 * slot + 3], h)):
                s_out[slot, r, h] = s_new
        new = new_states(k_ref, v_ref, s_in, h)
        if s_out is not None:
            for r, s_new in enumerate(new):
                s_out[len(earlier) // 3, r, h] = s_new
        q = q_ref[:, cols]
        wide = _dot(q.astype(BF16), jnp.concatenate([s.astype(BF16) for s in new], axis=1))
        wide = jnp.where(own, wide, 0.0)
        o = wide[:, :HEAD_DIM]
        for r in range(1, n):
            o = o + wide[:, r * HEAD_DIM:(r + 1) * HEAD_DIM]
        o = _unit_norm(o) * gn_ref[:, cols]
        o_ref[:, cols] = (_silu(g_ref[:, cols]) * o).astype(o_ref.dtype)


def _ret_sample(projs, gn_g, states, write_states):
    _, _, _, _, gamma = _retention_tables()
    j = len(projs) - 1
    n_rows = projs[-1].shape[0]
    n = SAMPLE_GROUP
    col = lambda name: pl.BlockSpec((n, RET_WIDTH), lambda i: (i, MIX_ORDER.index(name)))
    state = lambda slot: pl.BlockSpec((None, n, RET_HEADS, HEAD_DIM, HEAD_DIM), lambda i: (slot, i, 0, 0, 0))
    gn_g, gn_spec = _with_spec(gn_g)
    operands = [projs[-1]] * 4 + [gn_g, states]
    in_specs = [col("q"), col("k"), col("v"), col("g"), gn_spec, state(j)]
    out_specs = [pl.BlockSpec((n, RET_WIDTH), lambda i: (i, 0))]
    out_shape = [jax.ShapeDtypeStruct((n_rows, RET_WIDTH), BF16)]
    if write_states:
        for slot, proj in enumerate(projs[:-1]):
            operands += [proj, proj, states]
            in_specs += [col("k"), col("v"), state(slot)]
        out_specs.append(pl.BlockSpec((j + 1, n, RET_HEADS, HEAD_DIM, HEAD_DIM), lambda i: (0, i, 0, 0, 0)))
        out_shape.append(jax.ShapeDtypeStruct((j + 1,) + states.shape[1:], F32))
    return pl.pallas_call(
        functools.partial(_ret_sample_kernel, gamma),
        grid=(n_rows // n,),
        in_specs=in_specs, out_specs=out_specs, out_shape=out_shape,
        compiler_params=pltpu.CompilerParams(dimension_semantics=("parallel",),
                                             vmem_limit_bytes=VMEM_LIMIT),
    )(*operands)


def _lru_gates(uc, w_gates, b_gates, lam):
    pre = _dot(uc.astype(BF16), w_gates) + b_gates
    r = _sigmoid(pre[:, :LRU_WIDTH])
    i = _sigmoid(pre[:, LRU_WIDTH:])
    neg = -lam
    softplus = jnp.maximum(neg, 0.0) + jnp.log1p(jnp.exp(-jnp.abs(neg)))
    log_a = (-LRU_C * softplus) * r
    a = jnp.exp(log_a)
    xin = jnp.sqrt(jnp.tanh(-log_a) * (1.0 + a * a)) * (i * uc)
    return a, xin


def _lru_prompt_tile(u, ug, cw_ref, cb_ref, wg_ref, bg_ref, lam_ref, h, before, ext):
    n = SEQ_TILE
    pad = SUBLANES
    ext[0:pad, :] = before
    ext[pad:pad + n, :] = u
    uc = cb_ref[...] + cw_ref[CONV_W - 1:CONV_W, :] * u
    for j in range(1, CONV_W):
        uc = uc + cw_ref[CONV_W - 1 - j:CONV_W - j, :] * ext[pad - j:pad - j + n, :]

    a, x = _lru_gates(uc, wg_ref[...], bg_ref[...], lam_ref[...])

    groups = (n // SUBLANES, SUBLANES, LRU_WIDTH)
    a = a.reshape(groups)
    x = x.reshape(groups)
    step = lax.broadcasted_iota(jnp.int32, groups, 1)
    shift = 1
    while shift < SUBLANES:
        inside = step >= shift
        x = a * jnp.where(inside, pltpu.roll(x, shift, 1), 0.0) + x
        a = a * jnp.where(inside, pltpu.roll(a, shift, 1), 1.0)
        shift *= 2
    hs = []
    for i in range(n // SUBLANES):
        hg = a[i] * h + x[i]
        hs.append(hg)
        h = jnp.broadcast_to(hg[SUBLANES - 1:SUBLANES, :], (SUBLANES, LRU_WIDTH))
    return jnp.concatenate(hs, axis=0) * _gelu_tanh(ug), h, u[n - pad:, :]


QKVG_COLS = 2 * RET_WIDTH * 2
MIX_COLS = QKVG_COLS + 2 * LRU_WIDTH
MIX_ORDER = ("q", "k", "v", "g", "ux", "ug")
QK_START = MIX_ORDER.index("q") * RET_WIDTH
VG_START = MIX_ORDER.index("v") * RET_WIDTH
LRU_START = MIX_ORDER.index("ux") * RET_WIDTH


def _rotary_heads(qk, cos, sin):
    pieces = []
    for j in range(2 * RET_HEADS):
        blk = qk[:, j * HEAD_DIM:(j + 1) * HEAD_DIM]
        rot = blk * cos + pltpu.roll(blk, HEAD_DIM // 2, 1) * sin
        if j < RET_HEADS:
            rot = rot * (HEAD_DIM ** -0.5)
        pieces.append(rot)
    return jnp.concatenate(pieces, axis=1)


def _rotary_qk(proj, cos, sin):
    assert QK_START == 0
    return jnp.concatenate([_rotary_heads(proj[:, :VG_START], cos, sin), proj[:, VG_START:]], axis=1)


def _mixer_prompt_kernel(c_dec, x_ref, cos_ref, sin_ref, w_in, gn_ref, dec_ref, qd_ref, kd_ref,
                         cw_ref, cb_ref, wg_ref, bg_ref, lam_ref, w_ret, w_lru, ln_g, ln_b,
                         x2_ref, s_ref, h_ref, tail_ref, ext):
    heads = range(RET_HEADS)

    @pl.when(pl.program_id(1) == 0)
    def _():
        s_ref[...] = jnp.zeros_like(s_ref)
        h_ref[...] = jnp.zeros_like(h_ref)
        tail_ref[...] = jnp.zeros_like(tail_ref)

    x = x_ref[...]
    xb = x.astype(BF16)
    lru_in = _dot(xb, w_in[:, LRU_START:])
    qk_raw = _dot(xb, w_in[:, QK_START:VG_START])
    lru_out, h_new, tail = _lru_prompt_tile(lru_in[:, :LRU_WIDTH], lru_in[:, LRU_WIDTH:], cw_ref, cb_ref, wg_ref,
                                            bg_ref, lam_ref, h_ref[0], tail_ref[0], ext)
    vg = _dot(xb, w_in[:, VG_START:LRU_START])
    qk = _rotary_heads(qk_raw, cos_ref[...], sin_ref[...])
    ret_out, states = _ret_prompt_tile(
        c_dec, qk[:, :RET_WIDTH], qk[:, RET_WIDTH:], vg[:, :RET_WIDTH], vg[:, RET_WIDTH:],
        gn_ref, dec_ref, qd_ref, kd_ref, [s_ref[0, h] for h in heads])
    h_ref[0] = h_new
    tail_ref[0] = tail
    for h in heads:
        s_ref[0, h] = states[h]
    y = _dot(ret_out, w_ret[...]) + _dot(lru_out.astype(BF16), w_lru[...])
    x2_ref[...] = _layer_norm(DN_ALPHA * x + y, ln_g[...], ln_b[...])


def _mixer_prompt(x, cos, sin, consts, batch, seq):
    decay, q_dec, k_dec, c_dec, _ = _retention_tables()
    w_in, gn_g, *rest = consts
    consts = [w_in, gn_g, decay, q_dec, k_dec] + rest
    consts = [_with_spec(c) for c in consts]
    n_t = seq // SEQ_TILE
    tile = pl.BlockSpec((SEQ_TILE, D_MODEL), lambda b, t: (b * n_t + t, 0))
    table = pl.BlockSpec((SEQ_TILE, HEAD_DIM), lambda b, t: (t, 0))
    small = pl.BlockSpec((1, SUBLANES, LRU_WIDTH), lambda b, t: (b, 0, 0))
    return pl.pallas_call(
        functools.partial(_mixer_prompt_kernel, c_dec),
        grid=(batch, n_t),
        in_specs=[tile, table, table] + [spec for _, spec in consts],
        out_specs=[tile, pl.BlockSpec((1, RET_HEADS, HEAD_DIM, HEAD_DIM), lambda b, t: (b, 0, 0, 0)),
                   small, small],
        out_shape=[jax.ShapeDtypeStruct((batch * seq, D_MODEL), F32),
                   jax.ShapeDtypeStruct((batch, RET_HEADS, HEAD_DIM, HEAD_DIM), F32),
                   jax.ShapeDtypeStruct((batch, SUBLANES, LRU_WIDTH), F32),
                   jax.ShapeDtypeStruct((batch, SUBLANES, LRU_WIDTH), F32)],
        scratch_shapes=[pltpu.VMEM((SEQ_TILE + SUBLANES, LRU_WIDTH), F32)],
        compiler_params=pltpu.CompilerParams(dimension_semantics=("parallel", "arbitrary"),
                                             vmem_limit_bytes=VMEM_LIMIT),
    )(x, cos, sin, *[a for a, _ in consts])


def _lru_sample_kernel(ux_ref, ug_ref, b0_ref, b1_ref, b2_ref, h0_ref, cw_ref, cb_ref, wg_ref, bg_ref,
                       lam_ref, o_ref, h_ref):
    u = ux_ref[...]
    uc = (cb_ref[...] + cw_ref[3:4, :] * u + cw_ref[2:3, :] * b2_ref[...]
          + cw_ref[1:2, :] * b1_ref[...] + cw_ref[0:1, :] * b0_ref[...])
    a, x = _lru_gates(uc, wg_ref[...], bg_ref[...], lam_ref[...])
    h = a * h0_ref[...] + x
    h_ref[...] = h
    o_ref[...] = (h * _gelu_tanh(ug_ref[...])).astype(o_ref.dtype)


def _lru_sample(proj, conv_buf, h0, conv_w, conv_b, w_gates, b_gates, lam):
    n_rows = proj.shape[0]
    ux_col = MIX_ORDER.index("ux")
    col = lambda j: pl.BlockSpec((n_rows, LRU_WIDTH), lambda i: (0, j))
    bufs = [conv_buf[:, j, :] for j in range(CONV_W - 1)]
    rest = [_with_spec(c) for c in bufs + [h0, conv_w, conv_b, w_gates, b_gates, lam]]
    return pl.pallas_call(
        _lru_sample_kernel,
        grid=(1,),
        in_specs=[col(ux_col), col(ux_col + 1)] + [spec for _, spec in rest],
        out_specs=[pl.BlockSpec((n_rows, LRU_WIDTH), lambda i: (0, 0))] * 2,
        out_shape=[jax.ShapeDtypeStruct((n_rows, LRU_WIDTH), BF16),
                   jax.ShapeDtypeStruct((n_rows, LRU_WIDTH), F32)],
        compiler_params=pltpu.CompilerParams(vmem_limit_bytes=VMEM_LIMIT),
    )(proj, proj, *[a for a, _ in rest])


def _pool_finish(x, deltas, w_ref, pb_ref, ps_ref, g_ref, b_ref):
    ys = [_dot(deltas[:, gi * POOL_GROUP:(gi + 1) * POOL_GROUP], w_ref[gi]) for gi in range(len(POOL_WINDOWS))]
    y = (jnp.concatenate(ys, axis=1) + pb_ref[...]) * ps_ref[...]
    return _layer_norm(DN_ALPHA * x + y, g_ref[...], b_ref[...])


POOL_HALO = POOL_BUF + 1


POOL_ROWS = 64


def _pool_deltas(tiles_per_seq, tile, x_ref, halo, out, groups):
    n = ROW_TILE
    t = tile % tiles_per_seq
    for gi in groups:
        wnd = POOL_WINDOWS[gi]
        cols = slice(gi * POOL_GROUP, (gi + 1) * POOL_GROUP)
        for r0 in range(0, n, POOL_ROWS):
            rows = slice(r0, r0 + POOL_ROWS)
            if r0 == 0:
                before = jnp.where(t == 0, 0.0, halo[:, cols])
                w = jnp.concatenate([before, x_ref[rows, cols]], axis=0)
            else:
                w = x_ref[r0 - POOL_HALO:r0 + POOL_ROWS, cols]
            shift = 1
            while shift < wnd:
                w = w + pltpu.roll(w, shift, 0)
                shift *= 2
            pos = t * n + r0 + lax.broadcasted_iota(jnp.int32, (POOL_ROWS, POOL_GROUP), 0)
            cnt = jnp.minimum(wnd, pos + 1).astype(F32)
            out[rows, cols] = (w[POOL_HALO:, :] / cnt - x_ref[rows, cols]).astype(BF16)
        halo[:, cols] = x_ref[n - POOL_HALO:, cols]


def _pool_ln_prompt(tiles_per_seq, tile, rows, consts, scratch, parity, first):
    x_ref, x_next_ref = rows
    halo, stash_even, stash_odd = scratch
    stash = (stash_even, stash_odd)
    all_groups = range(len(POOL_WINDOWS))
    if first:
        halo[...] = jnp.zeros_like(halo)
        _pool_deltas(tiles_per_seq, tile, x_ref, halo, stash[parity], all_groups)
    x2 = _pool_finish(x_ref[...], stash[parity][...], *consts)
    ahead = functools.partial(_pool_deltas, tiles_per_seq, tile + 1, x_next_ref, halo, stash[1 - parity])
    return x2, (lambda: ahead(all_groups[2:]), lambda: ahead(all_groups[:2]))


def _pool_sample_kernel(x_ref, buf_ref, w_ref, pb_ref, ps_ref, g_ref, b_ref, o_ref):
    x = x_ref[...]
    deltas = []
    for gi, wnd in enumerate(POOL_WINDOWS):
        cols = slice(gi * POOL_GROUP, (gi + 1) * POOL_GROUP)
        s = x[:, cols]
        for j in range(1, wnd):
            s = s + buf_ref[POOL_BUF - j, :, cols]
        cnt = min(float(wnd), PAST_LEN + 1.0)
        deltas.append((s / cnt - x[:, cols]).astype(BF16))
    o_ref[...] = _pool_finish(x, jnp.concatenate(deltas, axis=1), w_ref, pb_ref, ps_ref, g_ref, b_ref)


def _pool_sample(x, buf_t, consts):
    full = lambda a: pl.BlockSpec(a.shape, lambda i: (0,) * a.ndim)
    consts = [_with_spec(c) for c in consts]
    return pl.pallas_call(
        _pool_sample_kernel,
        grid=(1,),
        in_specs=[full(x), full(buf_t)] + [spec for _, spec in consts],
        out_specs=full(x),
        out_shape=jax.ShapeDtypeStruct(x.shape, F32),
        compiler_params=pltpu.CompilerParams(vmem_limit_bytes=VMEM_LIMIT),
    )(x, buf_t, *[a for a, _ in consts])


def _rotary_tables(positions):
    half = HEAD_DIM // 2
    inv = ROPE_BASE ** (-jnp.arange(half, dtype=F32) / half)
    ang = positions.astype(F32)[:, None] * inv[None, :]
    cos, sin = jnp.cos(ang), jnp.sin(ang)
    return jnp.concatenate([cos, cos], axis=1), jnp.concatenate([-sin, sin], axis=1)


def _block_diag(w):
    layers, nb, n, _ = w.shape
    eye = jnp.eye(nb, dtype=w.dtype)
    return (eye[None, :, None, :, None] * w[:, :, :, None, :]).reshape(layers, nb * n, nb * n)


def kernel(x_prompt, x_sample, state_ret, state_lru_h, state_lru_conv, state_pool, w_ffn_gate, w_ffn_up, w_ffn_down, ln_g, ln_b, w_mix_in, w_mix_out, ret_gn_g, lru_conv_w, lru_conv_b, lru_wa, lru_ba, lru_wi, lru_bi, lru_lambda, pool_w, pool_b, pool_scale):
    batch, seq, _ = x_prompt.shape
    n_s = x_sample.shape[0]
    assert x_sample.shape[1] == 1 and seq % SEQ_TILE == 0 and n_s % SAMPLE_GROUP == 0
    xp = x_prompt.reshape(batch * seq, D_MODEL)
    xs = x_sample.reshape(n_s, D_MODEL)
    rows_of = lambda a: a.reshape(a.shape[:-1] + (1, a.shape[-1]))

    cos_p, sin_p = _rotary_tables(jnp.arange(seq))
    cos_s, sin_s = _rotary_tables(jnp.full((n_s,), PAST_LEN))

    w_in_all, w_out_all, pool_w_all = (w.astype(BF16) for w in (w_mix_in, w_mix_out, pool_w))
    ln_g_rows, ln_b_rows = rows_of(ln_g), rows_of(ln_b)
    ln = lambda layer, i: [_pick(ln_g_rows, layer, i), _pick(ln_b_rows, layer, i)]
    gn_rows, conv_b_rows, lam_rows = rows_of(ret_gn_g), rows_of(lru_conv_b), rows_of(lru_lambda)
    pool_b_rows, pool_scale_rows = rows_of(pool_b), rows_of(pool_scale)
    w_gates_all = jnp.concatenate([_block_diag(lru_wa), _block_diag(lru_wi)], axis=-1).astype(BF16)
    b_gates_rows = rows_of(jnp.concatenate([lru_ba, lru_bi], axis=-1))
    ffn_stacks = (w_ffn_gate, w_ffn_up, w_ffn_down)
    ffn_w = [w[0, 0].astype(BF16) for w in ffn_stacks]

    def ffn(pre_p, pre_s, rows_p, rows_s, pre_consts, layer, k, **kw):
        nonlocal ffn_w
        consts = ffn_w + ln(layer, 2 * k)
        nxt = (layer, 1) if k == 0 else (layer + 1, 0)
        cast_next = None if nxt[0] == DEPTH else (*ffn_stacks, *nxt)
        xp, xs, *cast = _ffn_pipeline(pre_p, pre_s, rows_p, rows_s, pre_consts, consts,
                                      cast_next=cast_next, **kw)
        ffn_w = cast
        return xp, xs

    rets_p, hs_p, convs_p, pools_p = [], [], [], []
    hs_s, convs_s, pools_s = [], [], []
    projs_s = []
    for layer in range(DEPTH):
        j = layer // 2
        xp, xs = ffn(_same(_identity_pre), _identity_pre, [xp], [xs], [], layer, 0)
        if layer % 2 == 0:
            gn = _pick(gn_rows, j)
            lru_consts = (_pick(lru_conv_w, j), _pick(conv_b_rows, j), _pick(w_gates_all, j),
                          _pick(b_gates_rows, j), _pick(lam_rows, j))
            out_consts = [_pick(w_out_all, j, rows=(RET_WIDTH, 0)), _pick(w_out_all, j, rows=(LRU_WIDTH, 1)),
                          *ln(layer, 1)]

            x1_s = xs
            xp, s_p, h_p, tail_p = _mixer_prompt(
                xp, cos_p, sin_p, [_pick(w_in_all, j), gn, *lru_consts, *out_consts], batch, seq)
            proj_s = _mix_in_sample(x1_s, cos_s, sin_s, _pick(w_in_all, j))
            projs_s.append(proj_s)
            ret_s, *rets_s = _ret_sample(projs_s, gn, state_ret, write_states=len(projs_s) == state_ret.shape[0])
            lru_s, h_s = _lru_sample(proj_s, state_lru_conv[j], _pick(state_lru_h, j), *lru_consts)

            rets_p.append(s_p)
            hs_p.append(h_p[:, 0, :])
            convs_p.append(tail_p[:, SUBLANES - (CONV_W - 1):, :])
            hs_s.append(h_s)
            convs_s.append(proj_s[:, MIX_ORDER.index("ux") * LRU_WIDTH:][:, :LRU_WIDTH])

            xp, xs = ffn(_same(_identity_pre), _mix_out_ln, [xp], [x1_s, ret_s, lru_s], out_consts, layer, 1)
        else:
            pool_consts = [_pick(pool_w_all, j), _pick(pool_b_rows, j), _pick(pool_scale_rows, j), *ln(layer, 1)]
            pools_p.append(xp.reshape(batch, seq, D_MODEL)[:, seq - POOL_BUF:, :])
            pools_s.append(xs)
            xs = _pool_sample(xs, jnp.transpose(state_pool[j], (1, 0, 2)), pool_consts)
            xp, xs = ffn(functools.partial(_pool_ln_prompt, seq // ROW_TILE), lambda rows, consts: rows[0],
                         [xp, xp], [xs], pool_consts, layer, 1, row_leads=[0, 1],
                         scratch=[pltpu.VMEM((POOL_HALO, D_MODEL), F32)]
                         + [pltpu.VMEM((ROW_TILE, D_MODEL), BF16)] * 2)

    shift_in = lambda buf, new: jnp.concatenate([buf[:, :, 1:, :], jnp.stack(new)[:, :, None, :]], axis=2)
    return (xp.reshape(batch, seq, D_MODEL), xs.reshape(n_s, 1, D_MODEL),
            jnp.stack(rets_p), jnp.stack(hs_p), jnp.stack(convs_p), jnp.stack(pools_p),
            rets_s[0], jnp.stack(hs_s), shift_in(state_lru_conv, convs_s), shift_in(state_pool, pools_s))
```
